```python
import math
import jax
import jax.numpy as jnp
from jax import lax
import numpy as np


D_MODEL = 1024
BATCH = 4
SEQ = 8192
DEPTH = 2

N_META = 16
POOL_GROUPS = 4
POOL_WINDOWS = (2, 4, 8, 16)
POOL_WIDTH = D_MODEL // 2
POOL_GROUP_DIM = POOL_WIDTH // POOL_GROUPS
DIFF_HEADS = 4
DIFF_QK_DIM = D_MODEL // 16
DIFF_V_DIM = 2 * DIFF_QK_DIM
Q_WIDTH = DIFF_HEADS * 2 * DIFF_QK_DIM
ATTN_WIDTH = DIFF_HEADS * DIFF_V_DIM
N_BRANCHES = 2
IN_COLS = POOL_WIDTH + 2 * Q_WIDTH + ATTN_WIDTH + N_BRANCHES * D_MODEL
REL_BUCKETS = 32
REL_MAX_DIST = 128
Q_BLOCK = 128
D_FF = 7 * D_MODEL // 2
N_EXPERTS = 8
TOP_K = 2
N_DENSE = (DEPTH + 1) // 2
N_MOE = DEPTH // 2
RMS_EPS = 1e-6
SUBLN_EPS = 1e-5

kernel_name = 'hybrid_pool_diffattn_moe_block'


def rmsnorm(x, g, eps=RMS_EPS):
    xf = x.astype(jnp.float32)
    y = xf * lax.rsqrt(jnp.mean(xf * xf, axis=-1, keepdims=True) + eps)
    return (y * g.astype(jnp.float32)).astype(x.dtype)


def t5_causal_bucket(q_pos, k_pos):
    n = jnp.maximum(q_pos[:, None] - k_pos[None, :], 0)
    max_exact = REL_BUCKETS // 2
    nf = jnp.maximum(n, 1).astype(jnp.float32)
    large = max_exact + (jnp.log(nf / max_exact) / math.log(REL_MAX_DIST / max_exact)
                         * (REL_BUCKETS - max_exact)).astype(jnp.int32)
    large = jnp.minimum(large, REL_BUCKETS - 1)
    return jnp.where(n < max_exact, n, large)


def pool_mixer(u, group_w, scale):
    b, L, _ = u.shape
    uf = u.astype(jnp.float32).reshape(b, L, POOL_GROUPS, POOL_GROUP_DIM)
    cs = jnp.pad(jnp.cumsum(uf, axis=1), ((0, 0), (1, 0), (0, 0), (0, 0)))
    t = jnp.arange(L)
    avgs = []
    for g, w in enumerate(POOL_WINDOWS):
        csg = cs[:, :, g]
        prev = jnp.pad(csg, ((0, 0), (w - 1, 0), (0, 0)))[:, :L]
        cnt = jnp.minimum(t + 1, w).astype(jnp.float32)[None, :, None]
        avgs.append((csg[:, 1:] - prev) / cnt)
    mixed = (jnp.stack(avgs, axis=2) - uf).astype(u.dtype)
    y = jnp.einsum('blgc,gcd->blgd', mixed, group_w)
    return y.reshape(b, L, POOL_WIDTH) * scale


def diff_attention(q, k, v, rel_bias, lam, lambda_init, subln_gain):
    b, L, H = q.shape[0], q.shape[1], q.shape[2]
    n_blocks = -(-L // Q_BLOCK)
    Lp = n_blocks * Q_BLOCK
    pad5 = ((0, 0), (0, Lp - L), (0, 0), (0, 0), (0, 0))
    q = jnp.pad(q, pad5) * (DIFF_QK_DIM ** -0.5)
    k = jnp.pad(k, pad5)
    v = jnp.pad(v, pad5[:4])
    k_pos = jnp.arange(Lp)
    q_blocks = q.reshape(b, n_blocks, Q_BLOCK, H, 2, DIFF_QK_DIM).transpose(1, 0, 2, 3, 4, 5)
    starts = jnp.arange(n_blocks) * Q_BLOCK

    def one_block(args):
        qb, s = args
        q_pos = s + jnp.arange(Q_BLOCK)
        bias = rel_bias[t5_causal_bucket(q_pos, k_pos)]
        bias = bias.transpose(2, 0, 1).astype(jnp.float32)
        logits = jnp.einsum('bqhcd,bkhcd->bhcqk', qb, k).astype(jnp.float32)
        logits = logits + bias[None, :, None]
        causal = k_pos[None, :] <= q_pos[:, None]
        logits = jnp.where(causal, logits, -jnp.inf)
        p = jax.nn.softmax(logits, axis=-1)
        a = p[:, :, 0] - lam * p[:, :, 1]
        return jnp.einsum('bhqk,bkhd->bqhd', a.astype(v.dtype), v)

    o = lax.map(one_block, (q_blocks, starts))
    o = o.transpose(1, 0, 2, 3, 4).reshape(b, Lp, H, DIFF_V_DIM)[:, :L]
    o = rmsnorm(o, subln_gain, SUBLN_EPS) * (1.0 - lambda_init)
    return o.reshape(b, L, H * DIFF_V_DIM)


def swiglu(h, wg, wu, wd):
    return (jax.nn.silu(h @ wg) * (h @ wu)) @ wd


def moe_swiglu(h, router, wg, wu, wd):
    logits = jnp.einsum('bld,de->ble', h, router).astype(jnp.float32)
    top_vals, top_idx = lax.top_k(logits, TOP_K)
    top_w = jax.nn.softmax(top_vals, axis=-1)
    combine = jnp.sum(jax.nn.one_hot(top_idx, N_EXPERTS, dtype=jnp.float32) * top_w[..., None], axis=-2)
    combine = combine.astype(h.dtype)
    out = jnp.zeros_like(h)
    for e in range(N_EXPERTS):
        out = out + combine[..., e:e + 1] * swiglu(h, wg[e], wu[e], wd[e])
    return out


def setup_inputs(seed: int = 0) -> dict:
    key = jax.random.key(seed)
    ks = jax.random.split(key, 24)
    f32 = jnp.float32

    def nrm(k, shape, scale):
        return jax.random.normal(k, shape, f32) * scale

    def gain(k, shape):
        return 1.0 + 0.02 * jax.random.normal(k, shape, f32)

    return {
        'x': nrm(ks[0], (BATCH, SEQ, D_MODEL), 1.0),
        'meta_tokens': nrm(ks[1], (N_META, D_MODEL), 1.0),
        'rel_bias': nrm(ks[2], (REL_BUCKETS, DIFF_HEADS), 0.5),
        'norm_mix': gain(ks[3], (DEPTH, D_MODEL)),
        'w_in': nrm(ks[4], (DEPTH, D_MODEL, IN_COLS), D_MODEL ** -0.5),
        'pool_group_w': nrm(ks[5], (DEPTH, POOL_GROUPS, POOL_GROUP_DIM, POOL_GROUP_DIM), POOL_GROUP_DIM ** -0.5),
        'pool_scale': 1.0 + 0.1 * jax.random.normal(ks[6], (DEPTH, POOL_WIDTH), f32),
        'lambda_q1': nrm(ks[7], (DEPTH, DIFF_QK_DIM), 0.1),
        'lambda_k1': nrm(ks[8], (DEPTH, DIFF_QK_DIM), 0.1),
        'lambda_q2': nrm(ks[9], (DEPTH, DIFF_QK_DIM), 0.1),
        'lambda_k2': nrm(ks[10], (DEPTH, DIFF_QK_DIM), 0.1),
        'subln_gain': gain(ks[11], (DEPTH, DIFF_V_DIM)),
        'w_pool_up': nrm(ks[12], (DEPTH, POOL_WIDTH, D_MODEL), POOL_WIDTH ** -0.5),
        'w_attn_up': nrm(ks[13], (DEPTH, ATTN_WIDTH, D_MODEL), ATTN_WIDTH ** -0.5),
        'w_out': nrm(ks[14], (DEPTH, D_MODEL, D_MODEL), D_MODEL ** -0.5),
        'norm_ffn': gain(ks[15], (DEPTH, D_MODEL)),
        'dense_w_gate': nrm(ks[16], (N_DENSE, D_MODEL, D_FF), D_MODEL ** -0.5),
        'dense_w_up': nrm(ks[17], (N_DENSE, D_MODEL, D_FF), D_MODEL ** -0.5),
        'dense_w_down': nrm(ks[18], (N_DENSE, D_FF, D_MODEL), D_FF ** -0.5),
        'moe_router': nrm(ks[19], (N_MOE, D_MODEL, N_EXPERTS), D_MODEL ** -0.5),
        'moe_w_gate': nrm(ks[20], (N_MOE, N_EXPERTS, D_MODEL, D_FF), D_MODEL ** -0.5),
        'moe_w_up': nrm(ks[21], (N_MOE, N_EXPERTS, D_MODEL, D_FF), D_MODEL ** -0.5),
        'moe_w_down': nrm(ks[22], (N_MOE, N_EXPERTS, D_FF, D_MODEL), D_FF ** -0.5),
        'final_norm': gain(ks[23], (D_MODEL,)),
    }


def reference(x, meta_tokens, rel_bias, norm_mix, w_in, pool_group_w, pool_scale,
              lambda_q1, lambda_k1, lambda_q2, lambda_k2, subln_gain,
              w_pool_up, w_attn_up, w_out, norm_ffn,
              dense_w_gate, dense_w_up, dense_w_down,
              moe_router, moe_w_gate, moe_w_up, moe_w_down, final_norm):
    b = x.shape[0]
    f32 = jnp.float32
    meta = jnp.broadcast_to(meta_tokens[None].astype(x.dtype), (b, N_META, D_MODEL))
    h = jnp.concatenate([meta, x], axis=1)
    L = h.shape[1]
    splits = [POOL_WIDTH, POOL_WIDTH + Q_WIDTH, POOL_WIDTH + 2 * Q_WIDTH,
              POOL_WIDTH + 2 * Q_WIDTH + ATTN_WIDTH]
    for layer in range(DEPTH):
        hn = rmsnorm(h, norm_mix[layer])
        proj = hn @ w_in[layer]
        u_pool, q, k, v, gate_logits = jnp.split(proj, splits, axis=-1)
        a_out = pool_mixer(u_pool, pool_group_w[layer], pool_scale[layer])
        lambda_init = 0.8 - 0.6 * math.exp(-0.3 * layer)
        lam = (jnp.exp(jnp.sum(lambda_q1[layer].astype(f32) * lambda_k1[layer].astype(f32)))
               - jnp.exp(jnp.sum(lambda_q2[layer].astype(f32) * lambda_k2[layer].astype(f32)))
               + lambda_init)
        q = q.reshape(b, L, DIFF_HEADS, 2, DIFF_QK_DIM)
        k = k.reshape(b, L, DIFF_HEADS, 2, DIFF_QK_DIM)
        v = v.reshape(b, L, DIFF_HEADS, DIFF_V_DIM)
        b_out = diff_attention(q, k, v, rel_bias, lam, lambda_init, subln_gain[layer]).astype(h.dtype)
        gates = jax.nn.sigmoid(gate_logits.astype(f32)).astype(h.dtype).reshape(b, L, N_BRANCHES, D_MODEL)
        merged = (gates[:, :, 0] * (a_out @ w_pool_up[layer])
                  + gates[:, :, 1] * (b_out @ w_attn_up[layer]))
        h = h + merged @ w_out[layer]
        hn = rmsnorm(h, norm_ffn[layer])
        j = layer // 2
        if layer % 2 == 0:
            f = swiglu(hn, dense_w_gate[j], dense_w_up[j], dense_w_down[j])
        else:
            f = moe_swiglu(hn, moe_router[j], moe_w_gate[j], moe_w_up[j], moe_w_down[j])
        h = h + f
    out = rmsnorm(h, final_norm)
    return out[:, N_META:]
```

```python
import functools
import math

import numpy as np
import jax
import jax.numpy as jnp
from jax import lax
from jax.experimental import pallas as pl
from jax.experimental.pallas import tpu as pltpu

F32 = jnp.float32
BF16 = jnp.bfloat16

N_META = 16
POOL_WINDOWS = (2, 4, 8, 16)
POOL_GROUP_DIM = 128
DIFF_HEADS = 4
DIFF_QK_DIM = 64
DIFF_V_DIM = 128
REL_BUCKETS = 32
REL_MAX_DIST = 128
TOP_K = 2
RMS_EPS = 1e-6
SUBLN_EPS = 1e-5

LANES = 128
TB = 768
HALO = 128
POOL_SUB = 256
FF_CHUNK = 512
MOE_TM = 512
GATHER_ROWS = 512
VMEM_LIMIT = 56 * 1024 * 1024
LOG2E = 1.4426950408889634
NEG = -1e30


def _cparams(*sem):
    return pltpu.CompilerParams(dimension_semantics=sem, vmem_limit_bytes=VMEM_LIMIT)


def _rms(x, g, eps):
    ms = jnp.mean(x * x, axis=-1, keepdims=True)
    return x * lax.rsqrt(ms + eps) * g


def _inproj_kernel(h_ref, g_ref, w_ref, u_ref, q_ref, k_ref, v_ref, gate_ref, *, pw, qw):
    hn = _rms(h_ref[...], g_ref[...], RMS_EPS).astype(BF16)

    def proj(lo, hi):
        return jnp.dot(hn, w_ref[:, lo:hi], preferred_element_type=F32)

    u_ref[...] = proj(0, pw).astype(BF16)
    q_ref[...] = (proj(pw, pw + qw) * (DIFF_QK_DIM ** -0.5 * LOG2E)).astype(BF16)
    k_ref[...] = proj(pw + qw, pw + 2 * qw).astype(BF16)
    v_ref[...] = proj(pw + 2 * qw, pw + 3 * qw).astype(BF16)
    gate_ref[...] = jax.nn.sigmoid(proj(pw + 3 * qw, w_ref.shape[1])).astype(BF16)


def _inproj(h, g, w, pw, qw):
    rows, d = h.shape
    cols = w.shape[1]
    gw = cols - pw - 3 * qw
    row_blk = lambda c: pl.BlockSpec((TB, c), lambda i: (i, 0))
    return pl.pallas_call(
        functools.partial(_inproj_kernel, pw=pw, qw=qw),
        grid=(rows // TB,),
        in_specs=[row_blk(d),
                  pl.BlockSpec((1, d), lambda i: (0, 0)),
                  pl.BlockSpec((d, cols), lambda i: (0, 0), pipeline_mode=pl.Buffered(1))],
        out_specs=[row_blk(pw), row_blk(qw), row_blk(qw), row_blk(qw), row_blk(gw)],
        out_shape=[jax.ShapeDtypeStruct((rows, c), BF16) for c in (pw, qw, qw, qw, gw)],
        compiler_params=_cparams("parallel"),
        name="inproj",
    )(h, g, w)


def _t5_bucket(n):
    max_exact = REL_BUCKETS // 2
    nf = jnp.maximum(n, 1).astype(F32)
    large = max_exact + (jnp.log(nf / max_exact) / math.log(REL_MAX_DIST / max_exact)
                         * (REL_BUCKETS - max_exact)).astype(jnp.int32)
    large = jnp.minimum(large, REL_BUCKETS - 1)
    return jnp.where(n < max_exact, n, large)


def _bias_tiles(rel_bias):
    assert TB >= REL_MAX_DIST
    tab = (rel_bias - rel_bias[REL_BUCKETS - 1]) * LOG2E
    i = jnp.arange(TB, dtype=jnp.int32)[:, None]
    j = jnp.arange(TB, dtype=jnp.int32)[None, :]
    n = i - j
    diag = jnp.where((n >= 0)[..., None], tab[_t5_bucket(jnp.maximum(n, 0))], NEG)
    prev = tab[_t5_bucket(n + TB)]
    return diag.transpose(2, 0, 1), prev.transpose(2, 0, 1)


def _attn_kernel(q_ref, k_ref, v_ref, dbias_ref, pbias_ref, lq1_ref, lk1_ref, lq2_ref, lk2_ref,
                 gain_ref, o_ref, m_ref, l_ref, acc_ref, *, lambda_init):
    i = pl.program_id(2)
    q = q_ref[...]
    lane = lax.broadcasted_iota(jnp.int32, q.shape, 1)
    zero = jnp.zeros_like(q)
    qs = (jnp.where(lane < DIFF_QK_DIM, q, zero), jnp.where(lane >= DIFF_QK_DIM, q, zero))

    m_ref[...] = jnp.full(m_ref.shape, NEG, F32)
    l_ref[...] = jnp.zeros(l_ref.shape, F32)
    acc_ref[...] = jnp.zeros(acc_ref.shape, F32)

    def step(j, bias):
        start = pl.multiple_of(j * TB, TB)
        kj = k_ref[pl.ds(start, TB), :]
        vj = v_ref[pl.ds(start, TB), :]
        for c in range(2):
            s = lax.dot_general(qs[c], kj, (((1,), (1,)), ((), ())), preferred_element_type=F32)
            if bias is not None:
                s = s + bias
            m_prev = m_ref[c]
            m_new = jnp.maximum(m_prev, jnp.max(s, axis=-1, keepdims=True))
            alpha = jnp.exp2(m_prev - m_new)
            p = jnp.exp2(s - m_new)
            l_ref[c] = alpha * l_ref[c] + jnp.sum(p, axis=-1, keepdims=True)
            acc_ref[c] = alpha * acc_ref[c] + jnp.dot(p.astype(BF16), vj, preferred_element_type=F32)
            m_ref[c] = m_new

    def far(j, carry):
        step(j, None)
        return carry

    lax.fori_loop(0, jnp.maximum(i - 1, 0), far, 0)

    @pl.when(i >= 1)
    def _():
        step(i - 1, pbias_ref[0])

    step(i, dbias_ref[0])

    lam = (jnp.exp(jnp.sum(lq1_ref[...] * lk1_ref[...], axis=-1, keepdims=True))
           - jnp.exp(jnp.sum(lq2_ref[...] * lk2_ref[...], axis=-1, keepdims=True))
           + lambda_init)
    o = acc_ref[0] / l_ref[0] - lam * (acc_ref[1] / l_ref[1])
    o_ref[...] = (_rms(o, gain_ref[...], SUBLN_EPS) * (1.0 - lambda_init)).astype(BF16)


def _attention(q, k, v, dbias, pbias, lq1, lk1, lq2, lk2, gain, *, batch, lp, lambda_init):
    rows, width = q.shape
    heads = width // (2 * DIFF_QK_DIM)
    nq = lp // TB
    qo_spec = pl.BlockSpec((TB, LANES), lambda b, h, i: (b * nq + i, h))
    kv_spec = pl.BlockSpec((lp, LANES), lambda b, h, i: (b, h))
    bias_spec = pl.BlockSpec((1, TB, TB), lambda b, h, i: (h, 0, 0))
    vec_spec = lambda n: pl.BlockSpec((1, n), lambda b, h, i: (0, 0))
    return pl.pallas_call(
        functools.partial(_attn_kernel, lambda_init=lambda_init),
        grid=(batch, heads, nq),
        in_specs=[qo_spec, kv_spec, kv_spec, bias_spec, bias_spec,
                  vec_spec(DIFF_QK_DIM), vec_spec(DIFF_QK_DIM), vec_spec(DIFF_QK_DIM),
                  vec_spec(DIFF_QK_DIM), vec_spec(DIFF_V_DIM)],
        out_specs=qo_spec,
        out_shape=jax.ShapeDtypeStruct((rows, heads * DIFF_V_DIM), BF16),
        scratch_shapes=[pltpu.VMEM((2, TB, 1), F32), pltpu.VMEM((2, TB, 1), F32),
                        pltpu.VMEM((2, TB, DIFF_V_DIM), F32)],
        compiler_params=_cparams("parallel", "parallel", "arbitrary"),
        name="diff_attn",
    )(q, k, v, dbias, pbias, lq1, lk1, lq2, lk2, gain)


def _pool_bands():
    r = np.arange(POOL_SUB)[:, None] + HALO
    c = np.arange(POOL_SUB + HALO)[None, :]
    d = r - c
    return np.stack([((d >= 0) & (d < w)).astype(np.float32) for w in POOL_WINDOWS])


def _merge_kernel(h_ref, u_ref, halo_ref, b_ref, gate_ref, band_ref, gw_ref, scale_ref,
                  wpu_ref, wau_ref, wo_ref, o_ref, *, nt):
    t = pl.program_id(0) % nt
    u = u_ref[...]
    halo = halo_ref[...]
    halo = jnp.where(t > 0, halo, jnp.zeros_like(halo))
    xcat = jnp.concatenate([halo, u], axis=0)
    uf = u.astype(F32)
    pos = t * TB + lax.broadcasted_iota(jnp.int32, (TB, 1), 0)
    d_model = o_ref.shape[1]

    ys = []
    for g, w in enumerate(POOL_WINDOWS):
        lo, hi = g * POOL_GROUP_DIM, (g + 1) * POOL_GROUP_DIM
        cnt = jnp.minimum(pos + 1, w).astype(F32)
        sums = [jnp.dot(band_ref[g], xcat[s * POOL_SUB:s * POOL_SUB + POOL_SUB + HALO, lo:hi],
                        preferred_element_type=F32) for s in range(TB // POOL_SUB)]
        mixed = (jnp.concatenate(sums, axis=0) / cnt - uf[:, lo:hi]).astype(BF16)
        ys.append(jnp.dot(mixed, gw_ref[g], preferred_element_type=F32))
    a_out = (jnp.concatenate(ys, axis=1) * scale_ref[...]).astype(BF16)
    pa = jnp.dot(a_out, wpu_ref[...], preferred_element_type=F32)
    pb = jnp.dot(b_ref[...], wau_ref[...], preferred_element_type=F32)
    gates = gate_ref[...]
    merged = (gates[:, :d_model].astype(F32) * pa + gates[:, d_model:].astype(F32) * pb).astype(BF16)
    o_ref[...] = h_ref[...] + jnp.dot(merged, wo_ref[...], preferred_element_type=F32)


def _merge(h, u, b_out, gates, bands, gw, scale, wpu, wau, wo, *, nt):
    rows, d = h.shape
    pw = u.shape[1]
    halo_per_tile = TB // HALO
    row_blk = lambda c: pl.BlockSpec((TB, c), lambda i: (i, 0))
    full = lambda a: pl.BlockSpec(a.shape, lambda i: (0,) * a.ndim)
    return pl.pallas_call(
        functools.partial(_merge_kernel, nt=nt),
        grid=(rows // TB,),
        in_specs=[row_blk(d), row_blk(pw),
                  pl.BlockSpec((HALO, pw), lambda i: (jnp.maximum(i * halo_per_tile - 1, 0), 0)),
                  row_blk(b_out.shape[1]), row_blk(gates.shape[1]),
                  full(bands), full(gw), full(scale), full(wpu), full(wau), full(wo)],
        out_specs=row_blk(d),
        out_shape=jax.ShapeDtypeStruct((rows, d), F32),
        compiler_params=_cparams("parallel"),
        name="pool_merge",
    )(h, u, u, b_out, gates, bands, gw, scale, wpu, wau, wo)


def _swiglu_chunk(x, wg, wu, wd):
    g = jnp.dot(x, wg, preferred_element_type=F32)
    u = jnp.dot(x, wu, preferred_element_type=F32)
    a = (g * jax.nn.sigmoid(g) * u).astype(BF16)
    return jnp.dot(a, wd, preferred_element_type=F32)


def _dense_ffn_kernel(h_ref, g_ref, wg_ref, wu_ref, wd_ref, o_ref, hn_ref, acc_ref):
    f = pl.program_id(1)

    @pl.when(f == 0)
    def _():
        hn_ref[...] = _rms(h_ref[...], g_ref[...], RMS_EPS).astype(BF16)
        acc_ref[...] = jnp.zeros(acc_ref.shape, F32)

    acc_ref[...] += _swiglu_chunk(hn_ref[...], wg_ref[...], wu_ref[...], wd_ref[...])

    @pl.when(f == pl.num_programs(1) - 1)
    def _():
        o_ref[...] = h_ref[...] + acc_ref[...]


def _dense_ffn(h, g, wg, wu, wd):
    rows, d = h.shape
    dff = wg.shape[1]
    return pl.pallas_call(
        _dense_ffn_kernel,
        grid=(rows // TB, dff // FF_CHUNK),
        in_specs=[pl.BlockSpec((TB, d), lambda i, f: (i, 0)),
                  pl.BlockSpec((1, d), lambda i, f: (0, 0)),
                  pl.BlockSpec((d, FF_CHUNK), lambda i, f: (0, f)),
                  pl.BlockSpec((d, FF_CHUNK), lambda i, f: (0, f)),
                  pl.BlockSpec((FF_CHUNK, d), lambda i, f: (f, 0))],
        out_specs=pl.BlockSpec((TB, d), lambda i, f: (i, 0)),
        out_shape=jax.ShapeDtypeStruct((rows, d), F32),
        scratch_shapes=[pltpu.VMEM((TB, d), BF16), pltpu.VMEM((TB, d), F32)],
        compiler_params=_cparams("parallel", "arbitrary"),
        name="dense_ffn",
    )(h, g, wg, wu, wd)


def _split_bf16(x):
    hi = x.astype(BF16)
    return hi, (x - hi.astype(F32)).astype(BF16)


def _router_kernel(h_ref, g_ref, r_ref, hn_ref, idx_ref, wt_ref, *, n_experts):
    hn = _rms(h_ref[...], g_ref[...], RMS_EPS)
    hn_ref[...] = hn
    x_hi, x_lo = _split_bf16(hn)
    r_hi, r_lo = _split_bf16(r_ref[...])
    logits = (jnp.dot(x_hi, r_hi, preferred_element_type=F32)
              + jnp.dot(x_lo, r_hi, preferred_element_type=F32)
              + jnp.dot(x_hi, r_lo, preferred_element_type=F32))
    lane = lax.broadcasted_iota(jnp.int32, logits.shape, 1)
    ninf = jnp.float32(-jnp.inf)
    lg = jnp.where(lane < n_experts, logits, ninf)
    v1 = jnp.max(lg, axis=-1, keepdims=True)
    i1 = jnp.min(jnp.where(lg == v1, lane, LANES), axis=-1, keepdims=True)
    lg2 = jnp.where(lane == i1, ninf, lg)
    v2 = jnp.max(lg2, axis=-1, keepdims=True)
    i2 = jnp.min(jnp.where(lg2 == v2, lane, LANES), axis=-1, keepdims=True)
    e = jnp.exp(v2 - v1)
    w1 = 1.0 / (1.0 + e)
    w2 = e / (1.0 + e)
    idx_ref[...] = jnp.where(lane == 0, i1, jnp.where(lane == 1, i2, 0))
    wt_ref[...] = jnp.where(lane == 0, w1, jnp.where(lane == 1, w2, 0.0))


def _router(h, g, router_padded, n_experts):
    rows, d = h.shape
    row_blk = lambda c: pl.BlockSpec((TB, c), lambda i: (i, 0))
    return pl.pallas_call(
        functools.partial(_router_kernel, n_experts=n_experts),
        grid=(rows // TB,),
        in_specs=[row_blk(d), pl.BlockSpec((1, d), lambda i: (0, 0)),
                  pl.BlockSpec((d, LANES), lambda i: (0, 0))],
        out_specs=[row_blk(d), row_blk(LANES), row_blk(LANES)],
        out_shape=[jax.ShapeDtypeStruct((rows, d), F32),
                   jax.ShapeDtypeStruct((rows, LANES), jnp.int32),
                   jax.ShapeDtypeStruct((rows, LANES), F32)],
        compiler_params=_cparams("parallel"),
        name="moe_router",
    )(h, g, router_padded)


def _row_copy(src_hbm, dst_ref, src_row, dst_row, sem):
    return pltpu.make_async_copy(src_hbm.at[pl.ds(src_row, 1), :], dst_ref.at[pl.ds(dst_row, 1), :], sem)


def _gather_into(idx_ref, src_hbm, dst_ref, sem, n_rows):
    def start(r, carry):
        _row_copy(src_hbm, dst_ref, idx_ref[0, 0, r], r, sem).start()
        return carry

    def wait(r, carry):
        _row_copy(src_hbm, dst_ref, 0, r, sem).wait()
        return carry

    lax.fori_loop(0, n_rows, start, 0, unroll=8)
    lax.fori_loop(0, n_rows, wait, 0, unroll=8)


def _gather_kernel(idx_ref, src_hbm, o_ref, sem):
    _gather_into(idx_ref, src_hbm, o_ref, sem, o_ref.shape[0])


def _gather_rows(src, idx):
    n = idx.shape[0]
    d = src.shape[1]
    return pl.pallas_call(
        _gather_kernel,
        grid=(n // GATHER_ROWS,),
        in_specs=[pl.BlockSpec((1, 1, GATHER_ROWS), lambda i: (i, 0, 0), memory_space=pltpu.SMEM),
                  pl.BlockSpec(memory_space=pl.ANY)],
        out_specs=pl.BlockSpec((GATHER_ROWS, d), lambda i: (i, 0)),
        out_shape=jax.ShapeDtypeStruct((n, d), src.dtype),
        scratch_shapes=[pltpu.SemaphoreType.DMA(())],
        compiler_params=_cparams("arbitrary"),
        name="moe_gather",
    )(idx.reshape(n // GATHER_ROWS, 1, GATHER_ROWS), src)


def _moe_ffn_kernel(te_ref, tv_ref, x_ref, rw_ref, wg_ref, wu_ref, wd_ref, o_ref, acc_ref):
    i, f = pl.program_id(0), pl.program_id(1)

    @pl.when(f == 0)
    def _():
        acc_ref[...] = jnp.zeros(acc_ref.shape, F32)

    @pl.when(tv_ref[i] > 0)
    def _():
        acc_ref[...] += _swiglu_chunk(x_ref[...].astype(BF16), wg_ref[0], wu_ref[0], wd_ref[0])

    @pl.when(f == pl.num_programs(1) - 1)
    def _():
        o_ref[...] = acc_ref[...] * rw_ref[...]


def _moe_ffn(tile_expert, tile_valid, xs, row_w, wg, wu, wd):
    rows, d = xs.shape
    dff = wg.shape[2]
    nf = dff // FF_CHUNK
    fidx = lambda f, i, tv: jnp.where(tv[i] > 0, f, nf - 1)
    grid_spec = pltpu.PrefetchScalarGridSpec(
        num_scalar_prefetch=2,
        grid=(rows // MOE_TM, nf),
        in_specs=[pl.BlockSpec((MOE_TM, d), lambda i, f, te, tv: (i, 0)),
                  pl.BlockSpec((MOE_TM, 1), lambda i, f, te, tv: (i, 0)),
                  pl.BlockSpec((1, d, FF_CHUNK), lambda i, f, te, tv: (te[i], 0, fidx(f, i, tv))),
                  pl.BlockSpec((1, d, FF_CHUNK), lambda i, f, te, tv: (te[i], 0, fidx(f, i, tv))),
                  pl.BlockSpec((1, FF_CHUNK, d), lambda i, f, te, tv: (te[i], fidx(f, i, tv), 0))],
        out_specs=pl.BlockSpec((MOE_TM, d), lambda i, f, te, tv: (i, 0)),
        scratch_shapes=[pltpu.VMEM((MOE_TM, d), F32)],
    )
    return pl.pallas_call(
        _moe_ffn_kernel,
        grid_spec=grid_spec,
        out_shape=jax.ShapeDtypeStruct((rows, d), F32),
        compiler_params=_cparams("arbitrary", "arbitrary"),
        name="moe_ffn",
    )(tile_expert, tile_valid, xs, row_w, wg, wu, wd)


def _combine_kernel(p1_ref, p2_ref, h_ref, ys_hbm, o_ref, y1_ref, y2_ref, sem):
    n = o_ref.shape[0]
    _gather_into(p1_ref, ys_hbm, y1_ref, sem.at[0], n)
    _gather_into(p2_ref, ys_hbm, y2_ref, sem.at[1], n)
    o_ref[...] = h_ref[...] + (y1_ref[...] + y2_ref[...])


def _combine(h, ys, pos1, pos2):
    rows, d = h.shape
    idx_spec = pl.BlockSpec((1, 1, GATHER_ROWS), lambda i: (i, 0, 0), memory_space=pltpu.SMEM)
    return pl.pallas_call(
        _combine_kernel,
        grid=(rows // GATHER_ROWS,),
        in_specs=[idx_spec, idx_spec,
                  pl.BlockSpec((GATHER_ROWS, d), lambda i: (i, 0)),
                  pl.BlockSpec(memory_space=pl.ANY)],
        out_specs=pl.BlockSpec((GATHER_ROWS, d), lambda i: (i, 0)),
        out_shape=jax.ShapeDtypeStruct((rows, d), F32),
        scratch_shapes=[pltpu.VMEM((GATHER_ROWS, d), F32), pltpu.VMEM((GATHER_ROWS, d), F32),
                        pltpu.SemaphoreType.DMA((2,))],
        compiler_params=_cparams("arbitrary"),
        name="moe_combine",
    )(pos1.reshape(rows // GATHER_ROWS, 1, GATHER_ROWS), pos2.reshape(rows // GATHER_ROWS, 1, GATHER_ROWS),
      h, ys)


def _moe_layer(h, g, router, wg, wu, wd):
    rows, d = h.shape
    n_experts = router.shape[1]
    router_padded = jnp.pad(router, ((0, 0), (0, LANES - n_experts)))
    hn, idx, wts = _router(h, g, router_padded, n_experts)

    n_assign = rows * TOP_K
    e_flat = idx[:, :TOP_K].reshape(n_assign)
    w_flat = wts[:, :TOP_K].reshape(n_assign)
    onehot = (e_flat[:, None] == jnp.arange(n_experts, dtype=jnp.int32)[None, :]).astype(jnp.int32)
    csum = jnp.cumsum(onehot, axis=0)
    rank = jnp.take_along_axis(csum, e_flat[:, None], axis=1)[:, 0] - 1
    counts = csum[-1]
    padded = ((counts + MOE_TM - 1) // MOE_TM) * MOE_TM
    ends = jnp.cumsum(padded)
    pos = (ends - padded)[e_flat] + rank
    n_tiles = n_assign // MOE_TM + n_experts
    sorted_rows = n_tiles * MOE_TM
    token = jnp.arange(n_assign, dtype=jnp.int32) // TOP_K
    row_token = jnp.zeros((sorted_rows,), jnp.int32).at[pos].set(token)
    row_w = jnp.zeros((sorted_rows,), F32).at[pos].set(w_flat)
    tile_start = jnp.arange(n_tiles, dtype=jnp.int32) * MOE_TM
    tile_expert = jnp.minimum(jnp.searchsorted(ends, tile_start, side="right"), n_experts - 1).astype(jnp.int32)
    tile_valid = (tile_start < ends[-1]).astype(jnp.int32)

    xs = _gather_rows(hn, row_token)
    ys = _moe_ffn(tile_expert, tile_valid, xs, row_w[:, None], wg, wu, wd)
    pos2d = pos.reshape(rows, TOP_K)
    return _combine(h, ys, pos2d[:, 0], pos2d[:, 1])


def _final_norm_kernel(h_ref, g_ref, o_ref):
    o_ref[...] = _rms(h_ref[...], g_ref[...], RMS_EPS)


def _final_norm(h, g):
    rows, d = h.shape
    return pl.pallas_call(
        _final_norm_kernel,
        grid=(rows // TB,),
        in_specs=[pl.BlockSpec((TB, d), lambda i: (i, 0)), pl.BlockSpec((1, d), lambda i: (0, 0))],
        out_specs=pl.BlockSpec((TB, d), lambda i: (i, 0)),
        out_shape=jax.ShapeDtypeStruct((rows, d), F32),
        compiler_params=_cparams("parallel"),
        name="final_norm",
    )(h, g)


def kernel(x, meta_tokens, rel_bias, norm_mix, w_in, pool_group_w, pool_scale, lambda_q1, lambda_k1, lambda_q2, lambda_k2, subln_gain, w_pool_up, w_attn_up, w_out, norm_ffn, dense_w_gate, dense_w_up, dense_w_down, moe_router, moe_w_gate, moe_w_up, moe_w_down, final_norm):
    batch, seq, d = x.shape
    depth = w_in.shape[0]
    seq_all = N_META + seq
    lp = -(-seq_all // TB) * TB
    nt = lp // TB
    pw = len(POOL_WINDOWS) * POOL_GROUP_DIM
    qw = DIFF_HEADS * 2 * DIFF_QK_DIM
    assert (batch * lp) % GATHER_ROWS == 0 and TB % HALO == 0 and TB % POOL_SUB == 0

    meta = jnp.broadcast_to(meta_tokens[None].astype(x.dtype), (batch, N_META, d))
    h = jnp.concatenate([meta, x, jnp.zeros((batch, lp - seq_all, d), x.dtype)], axis=1)
    h = h.reshape(batch * lp, d)

    dbias, pbias = _bias_tiles(rel_bias.astype(F32))
    bands = jnp.asarray(_pool_bands(), BF16)
    row = lambda a: a.reshape(1, -1).astype(F32)

    for layer in range(depth):
        u, q, k, v, gates = _inproj(h, row(norm_mix[layer]), w_in[layer].astype(BF16), pw, qw)
        lambda_init = 0.8 - 0.6 * math.exp(-0.3 * layer)
        b_out = _attention(q, k, v, dbias, pbias, row(lambda_q1[layer]), row(lambda_k1[layer]),
                           row(lambda_q2[layer]), row(lambda_k2[layer]), row(subln_gain[layer]),
                           batch=batch, lp=lp, lambda_init=lambda_init)
        h = _merge(h, u, b_out, gates, bands, pool_group_w[layer].astype(BF16), row(pool_scale[layer]),
                   w_pool_up[layer].astype(BF16), w_attn_up[layer].astype(BF16), w_out[layer].astype(BF16),
                   nt=nt)
        j = layer // 2
        if layer % 2 == 0:
            h = _dense_ffn(h, row(norm_ffn[layer]), dense_w_gate[j].astype(BF16),
                           dense_w_up[j].astype(BF16), dense_w_down[j].astype(BF16))
        else:
            h = _moe_layer(h, row(norm_ffn[layer]), moe_router[j].astype(F32), moe_w_gate[j].astype(BF16),
                           moe_w_up[j].astype(BF16), moe_w_down[j].astype(BF16))

    out = _final_norm(h, row(final_norm))
    return out.reshape(batch, lp, d)[:, N_META:seq_all]
```

```python
import functools
import math

import numpy as np
import jax
import jax.numpy as jnp
from jax import lax
from jax.experimental import pallas as pl
from jax.experimental.pallas import tpu as pltpu

F32 = jnp.float32
BF16 = jnp.bfloat16

N_META = 16
POOL_WINDOWS = (2, 4, 8, 16)
POOL_GROUP_DIM = 128
DIFF_HEADS = 4
DIFF_QK_DIM = 64
DIFF_V_DIM = 128
REL_BUCKETS = 32
REL_MAX_DIST = 128
TOP_K = 2
RMS_EPS = 1e-6
SUBLN_EPS = 1e-5

LANES = 128
TB = 768
HALO = 128
POOL_SUB = 256
FF_CHUNK = 512
MOE_TM = 512
GATHER_ROWS = 512
VMEM_LIMIT = 56 * 1024 * 1024
LOG2E = 1.4426950408889634
NEG = -1e30


def _cparams(*sem):
    return pltpu.CompilerParams(dimension_semantics=sem, vmem_limit_bytes=VMEM_LIMIT)


def _rms(x, g, eps):
    ms = jnp.mean(x * x, axis=-1, keepdims=True)
    return x * lax.rsqrt(ms + eps) * g


def _inproj_kernel(h_ref, g_ref, w_ref, u_ref, q_ref, k_ref, v_ref, gate_ref, *, pw, qw):
    hn = _rms(h_ref[...], g_ref[...], RMS_EPS).astype(BF16)

    def proj(lo, hi):
        return jnp.dot(hn, w_ref[:, lo:hi], preferred_element_type=F32)

    u_ref[...] = proj(0, pw).astype(BF16)
    q_ref[...] = (proj(pw, pw + qw) * (DIFF_QK_DIM ** -0.5 * LOG2E)).astype(BF16)
    k_ref[...] = proj(pw + qw, pw + 2 * qw).astype(BF16)
    v = proj(pw + 2 * qw, pw + 3 * qw).astype(BF16)
    ones = jnp.ones((v.shape[0], DIFF_V_DIM), BF16)
    v_ref[...] = jnp.concatenate(
        [blk for h in range(DIFF_HEADS) for blk in (v[:, h * DIFF_V_DIM:(h + 1) * DIFF_V_DIM], ones)], axis=1)
    gate_ref[...] = jax.nn.sigmoid(proj(pw + 3 * qw, w_ref.shape[1])).astype(BF16)


def _inproj(h, g, w, pw, qw):
    rows, d = h.shape
    cols = w.shape[1]
    gw = cols - pw - 3 * qw
    row_blk = lambda c: pl.BlockSpec((TB, c), lambda i: (i, 0))
    return pl.pallas_call(
        functools.partial(_inproj_kernel, pw=pw, qw=qw),
        grid=(rows // TB,),
        in_specs=[row_blk(d),
                  pl.BlockSpec((1, d), lambda i: (0, 0)),
                  pl.BlockSpec((d, cols), lambda i: (0, 0), pipeline_mode=pl.Buffered(1))],
        out_specs=[row_blk(pw), row_blk(qw), row_blk(qw), row_blk(2 * qw), row_blk(gw)],
        out_shape=[jax.ShapeDtypeStruct((rows, c), BF16) for c in (pw, qw, qw, 2 * qw, gw)],
        compiler_params=_cparams("parallel"),
        name="inproj",
    )(h, g, w)


def _bucket_starts():
    max_exact = REL_BUCKETS // 2
    n = np.arange(1, REL_MAX_DIST + 1)
    large = max_exact + (np.log(n.astype(np.float32) / max_exact) / np.float32(math.log(REL_MAX_DIST / max_exact))
                         * (REL_BUCKETS - max_exact)).astype(np.int32)
    bucket = np.where(n < max_exact, n, np.minimum(large, REL_BUCKETS - 1))
    return [0] + [int(n[bucket >= b].min()) for b in range(1, REL_BUCKETS)]


def _bias_tiles_kernel(tab_ref, diag_ref, prev_ref, *, starts):
    h = pl.program_id(0)
    far = tab_ref[REL_BUCKETS - 1, h]
    row = lax.broadcasted_iota(jnp.int32, (TB, TB), 0)
    col = lax.broadcasted_iota(jnp.int32, (TB, TB), 1)
    dist = row - col

    def bias_of(n):
        t = jnp.full(n.shape, (tab_ref[0, h] - far) * LOG2E, F32)
        for b in range(1, REL_BUCKETS):
            t = jnp.where(n >= starts[b], (tab_ref[b, h] - far) * LOG2E, t)
        return t

    diag_ref[0] = jnp.where(dist >= 0, bias_of(dist), NEG)
    prev_ref[0] = bias_of(dist + TB)


def _bias_tiles(rel_bias):
    starts = _bucket_starts()
    assert TB >= starts[-1]
    heads = rel_bias.shape[1]
    tile = pl.BlockSpec((1, TB, TB), lambda h: (h, 0, 0))
    return pl.pallas_call(
        functools.partial(_bias_tiles_kernel, starts=starts),
        grid=(heads,),
        in_specs=[pl.BlockSpec(memory_space=pltpu.SMEM)],
        out_specs=[tile, tile],
        out_shape=[jax.ShapeDtypeStruct((heads, TB, TB), F32)] * 2,
        compiler_params=_cparams("parallel"),
        name="bias_tiles",
    )(rel_bias)


def _attn_kernel(q_ref, k_ref, v_ref, dbias_ref, pbias_ref, lq1_ref, lk1_ref, lq2_ref, lk2_ref,
                 gain_ref, o_ref, m_ref, acc_ref, *, lambda_init):
    i = pl.program_id(2)
    q = q_ref[...]
    lane = lax.broadcasted_iota(jnp.int32, q.shape, 1)
    zero = jnp.zeros_like(q)
    qs = (jnp.where(lane < DIFF_QK_DIM, q, zero), jnp.where(lane >= DIFF_QK_DIM, q, zero))

    m_ref[...] = jnp.full(m_ref.shape, NEG, F32)
    acc_ref[...] = jnp.zeros(acc_ref.shape, F32)

    def step(j, bias_ref):
        start = pl.multiple_of(j * TB, TB)
        kj = k_ref[pl.ds(start, TB), :]
        vj = v_ref[pl.ds(start, TB), :]
        for c in range(2):
            s = lax.dot_general(qs[c], kj, (((1,), (1,)), ((), ())), preferred_element_type=F32)
            if bias_ref is not None:
                s = s + bias_ref[0]
            cols = [s[:, b * LANES:(b + 1) * LANES] for b in range(TB // LANES)]
            m_prev = m_ref[c]
            m_cur = jnp.max(functools.reduce(jnp.maximum, cols), axis=-1, keepdims=True)
            m_new = jnp.maximum(m_prev, m_cur)
            alpha = jnp.exp2(m_prev - m_new)
            p = jnp.concatenate([jnp.exp2(blk - m_new).astype(BF16) for blk in cols], axis=1)
            pv = jnp.dot(p, vj, preferred_element_type=F32)
            acc_ref[c] = jnp.concatenate([alpha, alpha], axis=1) * acc_ref[c] + pv
            m_ref[c] = m_new

    def far(j, carry):
        step(j, None)
        return carry

    lax.fori_loop(0, jnp.maximum(i - 1, 0), far, 0)

    @pl.when(i >= 1)
    def _():
        step(i - 1, pbias_ref)

    step(i, dbias_ref)

    lam = (jnp.exp(jnp.sum(lq1_ref[...] * lk1_ref[...], axis=-1, keepdims=True))
           - jnp.exp(jnp.sum(lq2_ref[...] * lk2_ref[...], axis=-1, keepdims=True))
           + lambda_init)
    o = (acc_ref[0, :, :DIFF_V_DIM] / acc_ref[0, :, DIFF_V_DIM:]
         - lam * (acc_ref[1, :, :DIFF_V_DIM] / acc_ref[1, :, DIFF_V_DIM:]))
    o_ref[...] = (_rms(o, gain_ref[...], SUBLN_EPS) * (1.0 - lambda_init)).astype(BF16)


def _attention(q, k, v, dbias, pbias, lq1, lk1, lq2, lk2, gain, *, batch, lp, lambda_init):
    rows, width = q.shape
    heads = width // (2 * DIFF_QK_DIM)
    nq = lp // TB
    qo_spec = pl.BlockSpec((TB, LANES), lambda b, h, i: (b * nq + i, h))
    k_spec = pl.BlockSpec((lp, 2 * DIFF_QK_DIM), lambda b, h, i: (b, h))
    v_spec = pl.BlockSpec((lp, 2 * DIFF_V_DIM), lambda b, h, i: (b, h))
    bias_spec = pl.BlockSpec((1, TB, TB), lambda b, h, i: (h, 0, 0))
    vec_spec = lambda n: pl.BlockSpec((1, n), lambda b, h, i: (0, 0))
    return pl.pallas_call(
        functools.partial(_attn_kernel, lambda_init=lambda_init),
        grid=(batch, heads, nq),
        in_specs=[qo_spec, k_spec, v_spec, bias_spec, bias_spec,
                  vec_spec(DIFF_QK_DIM), vec_spec(DIFF_QK_DIM), vec_spec(DIFF_QK_DIM),
                  vec_spec(DIFF_QK_DIM), vec_spec(DIFF_V_DIM)],
        out_specs=qo_spec,
        out_shape=jax.ShapeDtypeStruct((rows, heads * DIFF_V_DIM), BF16),
        scratch_shapes=[pltpu.VMEM((2, TB, LANES), F32), pltpu.VMEM((2, TB, 2 * DIFF_V_DIM), F32)],
        compiler_params=_cparams("parallel", "parallel", "arbitrary"),
        name="diff_attn",
    )(q, k, v, dbias, pbias, lq1, lk1, lq2, lk2, gain)


def _pool_bands():
    r = np.arange(POOL_SUB)[:, None] + HALO
    c = np.arange(POOL_SUB + HALO)[None, :]
    d = r - c
    return np.stack([((d >= 0) & (d < w)).astype(np.float32) for w in POOL_WINDOWS])


def _merge_kernel(h_ref, u_ref, halo_ref, b_ref, gate_ref, band_ref, gw_ref, scale_ref,
                  wpu_ref, wau_ref, wo_ref, o_ref, *, nt):
    t = pl.program_id(0) % nt
    u = u_ref[...]
    halo = halo_ref[...]
    halo = jnp.where(t > 0, halo, jnp.zeros_like(halo))
    xcat = jnp.concatenate([halo, u], axis=0)
    uf = u.astype(F32)
    pos = t * TB + lax.broadcasted_iota(jnp.int32, (TB, 1), 0)
    d_model = o_ref.shape[1]

    ys = []
    for g, w in enumerate(POOL_WINDOWS):
        lo, hi = g * POOL_GROUP_DIM, (g + 1) * POOL_GROUP_DIM
        cnt = jnp.minimum(pos + 1, w).astype(F32)
        sums = [jnp.dot(band_ref[g], xcat[s * POOL_SUB:s * POOL_SUB + POOL_SUB + HALO, lo:hi],
                        preferred_element_type=F32) for s in range(TB // POOL_SUB)]
        mixed = (jnp.concatenate(sums, axis=0) / cnt - uf[:, lo:hi]).astype(BF16)
        ys.append(jnp.dot(mixed, gw_ref[g], preferred_element_type=F32))
    a_out = (jnp.concatenate(ys, axis=1) * scale_ref[...]).astype(BF16)
    pa = jnp.dot(a_out, wpu_ref[...], preferred_element_type=F32)
    pb = jnp.dot(b_ref[...], wau_ref[...], preferred_element_type=F32)
    gates = gate_ref[...]
    merged = (gates[:, :d_model].astype(F32) * pa + gates[:, d_model:].astype(F32) * pb).astype(BF16)
    o_ref[...] = h_ref[...] + jnp.dot(merged, wo_ref[...], preferred_element_type=F32)


def _merge(h, u, b_out, gates, bands, gw, scale, wpu, wau, wo, *, nt):
    rows, d = h.shape
    pw = u.shape[1]
    halo_per_tile = TB // HALO
    row_blk = lambda c: pl.BlockSpec((TB, c), lambda i: (i, 0))
    full = lambda a: pl.BlockSpec(a.shape, lambda i: (0,) * a.ndim)
    return pl.pallas_call(
        functools.partial(_merge_kernel, nt=nt),
        grid=(rows // TB,),
        in_specs=[row_blk(d), row_blk(pw),
                  pl.BlockSpec((HALO, pw), lambda i: (jnp.maximum(i * halo_per_tile - 1, 0), 0)),
                  row_blk(b_out.shape[1]), row_blk(gates.shape[1]),
                  full(bands), full(gw), full(scale), full(wpu), full(wau), full(wo)],
        out_specs=row_blk(d),
        out_shape=jax.ShapeDtypeStruct((rows, d), F32),
        compiler_params=_cparams("parallel"),
        name="pool_merge",
    )(h, u, u, b_out, gates, bands, gw, scale, wpu, wau, wo)


def _swiglu_chunk(x, wg, wu, wd):
    g = jnp.dot(x, wg, preferred_element_type=F32)
    u = jnp.dot(x, wu, preferred_element_type=F32)
    a = (g * jax.nn.sigmoid(g) * u).astype(BF16)
    return jnp.dot(a, wd, preferred_element_type=F32)


def _dense_ffn_kernel(h_ref, g_ref, wg_ref, wu_ref, wd_ref, o_ref, hn_ref, acc_ref):
    f = pl.program_id(1)

    @pl.when(f == 0)
    def _():
        hn_ref[...] = _rms(h_ref[...], g_ref[...], RMS_EPS).astype(BF16)
        acc_ref[...] = jnp.zeros(acc_ref.shape, F32)

    acc_ref[...] += _swiglu_chunk(hn_ref[...], wg_ref[...], wu_ref[...], wd_ref[...])

    @pl.when(f == pl.num_programs(1) - 1)
    def _():
        o_ref[...] = h_ref[...] + acc_ref[...]


def _dense_ffn(h, g, wg, wu, wd):
    rows, d = h.shape
    dff = wg.shape[1]
    return pl.pallas_call(
        _dense_ffn_kernel,
        grid=(rows // TB, dff // FF_CHUNK),
        in_specs=[pl.BlockSpec((TB, d), lambda i, f: (i, 0)),
                  pl.BlockSpec((1, d), lambda i, f: (0, 0)),
                  pl.BlockSpec((d, FF_CHUNK), lambda i, f: (0, f)),
                  pl.BlockSpec((d, FF_CHUNK), lambda i, f: (0, f)),
                  pl.BlockSpec((FF_CHUNK, d), lambda i, f: (f, 0))],
        out_specs=pl.BlockSpec((TB, d), lambda i, f: (i, 0)),
        out_shape=jax.ShapeDtypeStruct((rows, d), F32),
        scratch_shapes=[pltpu.VMEM((TB, d), BF16), pltpu.VMEM((TB, d), F32)],
        compiler_params=_cparams("parallel", "arbitrary"),
        name="dense_ffn",
    )(h, g, wg, wu, wd)


def _split_bf16(x):
    hi = x.astype(BF16)
    return hi, (x - hi.astype(F32)).astype(BF16)


def _router_kernel(h_ref, g_ref, r_ref, hn_ref, idx_ref, wt_ref, *, n_experts):
    hn = _rms(h_ref[...], g_ref[...], RMS_EPS)
    hn_ref[...] = hn
    x_hi, x_lo = _split_bf16(hn)
    r_hi, r_lo = _split_bf16(r_ref[...])
    logits = (jnp.dot(x_hi, r_hi, preferred_element_type=F32)
              + jnp.dot(x_lo, r_hi, preferred_element_type=F32)
              + jnp.dot(x_hi, r_lo, preferred_element_type=F32))
    lane = lax.broadcasted_iota(jnp.int32, logits.shape, 1)
    ninf = jnp.float32(-jnp.inf)
    lg = jnp.where(lane < n_experts, logits, ninf)
    v1 = jnp.max(lg, axis=-1, keepdims=True)
    i1 = jnp.min(jnp.where(lg == v1, lane, LANES), axis=-1, keepdims=True)
    lg2 = jnp.where(lane == i1, ninf, lg)
    v2 = jnp.max(lg2, axis=-1, keepdims=True)
    i2 = jnp.min(jnp.where(lg2 == v2, lane, LANES), axis=-1, keepdims=True)
    e = jnp.exp(v2 - v1)
    w1 = 1.0 / (1.0 + e)
    w2 = e / (1.0 + e)
    idx_ref[...] = jnp.where(lane == 0, i1, jnp.where(lane == 1, i2, 0))
    wt_ref[...] = jnp.where(lane == 0, w1, jnp.where(lane == 1, w2, 0.0))


def _router(h, g, router_padded, n_experts):
    rows, d = h.shape
    row_blk = lambda c: pl.BlockSpec((TB, c), lambda i: (i, 0))
    return pl.pallas_call(
        functools.partial(_router_kernel, n_experts=n_experts),
        grid=(rows // TB,),
        in_specs=[row_blk(d), pl.BlockSpec((1, d), lambda i: (0, 0)),
                  pl.BlockSpec((d, LANES), lambda i: (0, 0))],
        out_specs=[row_blk(d), row_blk(LANES), row_blk(LANES)],
        out_shape=[jax.ShapeDtypeStruct((rows, d), F32),
                   jax.ShapeDtypeStruct((rows, LANES), jnp.int32),
                   jax.ShapeDtypeStruct((rows, LANES), F32)],
        compiler_params=_cparams("parallel"),
        name="moe_router",
    )(h, g, router_padded)


def _row_copy(src_ref, dst_ref, src_row, dst_row, sem):
    return pltpu.make_async_copy(src_ref.at[pl.ds(src_row, 1), :], dst_ref.at[pl.ds(dst_row, 1), :], sem)


def _for_rows(n_rows, fn):
    def body(r, carry):
        fn(r)
        return carry

    lax.fori_loop(0, n_rows, body, 0, unroll=8)


def _idx_blocks(idx):
    return idx.reshape(idx.shape[0] // GATHER_ROWS, 1, GATHER_ROWS)


_IDX_SPEC = pl.BlockSpec((1, 1, GATHER_ROWS), lambda i: (i, 0, 0), memory_space=pltpu.SMEM)


def _dispatch_kernel(p1_ref, p2_ref, hn_ref, xs_init_hbm, xs_hbm, sem):
    del xs_init_hbm
    n = hn_ref.shape[0]

    def start(r):
        _row_copy(hn_ref, xs_hbm, r, p1_ref[0, 0, r], sem.at[0]).start()
        _row_copy(hn_ref, xs_hbm, r, p2_ref[0, 0, r], sem.at[1]).start()

    def wait(r):
        _row_copy(hn_ref, xs_hbm, r, 0, sem.at[0]).wait()
        _row_copy(hn_ref, xs_hbm, r, 0, sem.at[1]).wait()

    _for_rows(n, start)
    _for_rows(n, wait)


def _dispatch(hn, pos1, pos2, sorted_rows):
    rows, d = hn.shape
    return pl.pallas_call(
        _dispatch_kernel,
        grid=(rows // GATHER_ROWS,),
        in_specs=[_IDX_SPEC, _IDX_SPEC, pl.BlockSpec((GATHER_ROWS, d), lambda i: (i, 0)),
                  pl.BlockSpec(memory_space=pl.ANY)],
        out_specs=pl.BlockSpec(memory_space=pl.ANY),
        out_shape=jax.ShapeDtypeStruct((sorted_rows, d), hn.dtype),
        scratch_shapes=[pltpu.SemaphoreType.DMA((2,))],
        input_output_aliases={3: 0},
        compiler_params=_cparams("arbitrary"),
        name="moe_dispatch",
    )(_idx_blocks(pos1), _idx_blocks(pos2), hn, jnp.zeros((sorted_rows, d), hn.dtype))


def _moe_ffn_kernel(te_ref, tv_ref, x_ref, wg_ref, wu_ref, wd_ref, o_ref, acc_ref):
    i, f = pl.program_id(0), pl.program_id(1)

    @pl.when(f == 0)
    def _():
        acc_ref[...] = jnp.zeros(acc_ref.shape, F32)

    @pl.when(tv_ref[i] > 0)
    def _():
        acc_ref[...] += _swiglu_chunk(x_ref[...].astype(BF16), wg_ref[0], wu_ref[0], wd_ref[0])

    @pl.when(f == pl.num_programs(1) - 1)
    def _():
        o_ref[...] = acc_ref[...]


def _moe_ffn(tile_expert, tile_valid, xs, wg, wu, wd):
    rows, d = xs.shape
    dff = wg.shape[2]
    nf = dff // FF_CHUNK
    fidx = lambda f, i, tv: jnp.where(tv[i] > 0, f, nf - 1)
    grid_spec = pltpu.PrefetchScalarGridSpec(
        num_scalar_prefetch=2,
        grid=(rows // MOE_TM, nf),
        in_specs=[pl.BlockSpec((MOE_TM, d), lambda i, f, te, tv: (i, 0)),
                  pl.BlockSpec((1, d, FF_CHUNK), lambda i, f, te, tv: (te[i], 0, fidx(f, i, tv))),
                  pl.BlockSpec((1, d, FF_CHUNK), lambda i, f, te, tv: (te[i], 0, fidx(f, i, tv))),
                  pl.BlockSpec((1, FF_CHUNK, d), lambda i, f, te, tv: (te[i], fidx(f, i, tv), 0))],
        out_specs=pl.BlockSpec((MOE_TM, d), lambda i, f, te, tv: (i, 0)),
        scratch_shapes=[pltpu.VMEM((MOE_TM, d), F32)],
    )
    return pl.pallas_call(
        _moe_ffn_kernel,
        grid_spec=grid_spec,
        out_shape=jax.ShapeDtypeStruct((rows, d), F32),
        compiler_params=_cparams("arbitrary", "arbitrary"),
        name="moe_ffn",
    )(tile_expert, tile_valid, xs, wg, wu, wd)


def _combine_kernel(p1_ref, p2_ref, h_ref, wt_ref, ys_hbm, g_ref, o_ref, y1_ref, y2_ref, sem, *, final):
    n = o_ref.shape[0]

    def start(r):
        _row_copy(ys_hbm, y1_ref, p1_ref[0, 0, r], r, sem.at[0]).start()
        _row_copy(ys_hbm, y2_ref, p2_ref[0, 0, r], r, sem.at[1]).start()

    def wait(r):
        _row_copy(ys_hbm, y1_ref, 0, r, sem.at[0]).wait()
        _row_copy(ys_hbm, y2_ref, 0, r, sem.at[1]).wait()

    _for_rows(n, start)
    _for_rows(n, wait)
    wt = wt_ref[...]
    out = h_ref[...] + (wt[:, 0:1] * y1_ref[...] + wt[:, 1:2] * y2_ref[...])
    o_ref[...] = _rms(out, g_ref[...], RMS_EPS) if final else out


def _combine(h, wts, ys, pos1, pos2, final_gain):
    rows, d = h.shape
    final = final_gain is not None
    gain = final_gain if final else jnp.ones((1, d), F32)
    row_blk = lambda c: pl.BlockSpec((GATHER_ROWS, c), lambda i: (i, 0))
    return pl.pallas_call(
        functools.partial(_combine_kernel, final=final),
        grid=(rows // GATHER_ROWS,),
        in_specs=[_IDX_SPEC, _IDX_SPEC, row_blk(d), row_blk(LANES),
                  pl.BlockSpec(memory_space=pl.ANY), pl.BlockSpec((1, d), lambda i: (0, 0))],
        out_specs=row_blk(d),
        out_shape=jax.ShapeDtypeStruct((rows, d), F32),
        scratch_shapes=[pltpu.VMEM((GATHER_ROWS, d), F32), pltpu.VMEM((GATHER_ROWS, d), F32),
                        pltpu.SemaphoreType.DMA((2,))],
        compiler_params=_cparams("arbitrary"),
        name="moe_combine",
    )(_idx_blocks(pos1), _idx_blocks(pos2), h, wts, ys, gain)


def _moe_layer(h, g, router, wg, wu, wd, final_gain):
    rows, d = h.shape
    n_experts = router.shape[1]
    router_padded = jnp.pad(router, ((0, 0), (0, LANES - n_experts)))
    hn, idx, wts = _router(h, g, router_padded, n_experts)

    n_assign = rows * TOP_K
    e_flat = idx[:, :TOP_K].reshape(n_assign)
    onehot = (e_flat[:, None] == jnp.arange(n_experts, dtype=jnp.int32)[None, :]).astype(jnp.int32)
    csum = jnp.cumsum(onehot, axis=0)
    rank = jnp.take_along_axis(csum, e_flat[:, None], axis=1)[:, 0] - 1
    counts = csum[-1]
    padded = ((counts + MOE_TM - 1) // MOE_TM) * MOE_TM
    ends = jnp.cumsum(padded)
    pos = (ends - padded)[e_flat] + rank
    n_tiles = n_assign // MOE_TM + n_experts
    tile_start = jnp.arange(n_tiles, dtype=jnp.int32) * MOE_TM
    tile_expert = jnp.minimum(jnp.searchsorted(ends, tile_start, side="right"), n_experts - 1).astype(jnp.int32)
    tile_valid = (tile_start < ends[-1]).astype(jnp.int32)

    pos2d = pos.reshape(rows, TOP_K)
    xs = _dispatch(hn, pos2d[:, 0], pos2d[:, 1], n_tiles * MOE_TM)
    ys = _moe_ffn(tile_expert, tile_valid, xs, wg, wu, wd)
    return _combine(h, wts, ys, pos2d[:, 0], pos2d[:, 1], final_gain)


def _final_norm_kernel(h_ref, g_ref, o_ref):
    o_ref[...] = _rms(h_ref[...], g_ref[...], RMS_EPS)


def _final_norm(h, g):
    rows, d = h.shape
    return pl.pallas_call(
        _final_norm_kernel,
        grid=(rows // TB,),
        in_specs=[pl.BlockSpec((TB, d), lambda i: (i, 0)), pl.BlockSpec((1, d), lambda i: (0, 0))],
        out_specs=pl.BlockSpec((TB, d), lambda i: (i, 0)),
        out_shape=jax.ShapeDtypeStruct((rows, d), F32),
        compiler_params=_cparams("parallel"),
        name="final_norm",
    )(h, g)


def kernel(x, meta_tokens, rel_bias, norm_mix, w_in, pool_group_w, pool_scale, lambda_q1, lambda_k1, lambda_q2, lambda_k2, subln_gain, w_pool_up, w_attn_up, w_out, norm_ffn, dense_w_gate, dense_w_up, dense_w_down, moe_router, moe_w_gate, moe_w_up, moe_w_down, final_norm):
    batch, seq, d = x.shape
    depth = w_in.shape[0]
    seq_all = N_META + seq
    lp = -(-seq_all // TB) * TB
    nt = lp // TB
    pw = len(POOL_WINDOWS) * POOL_GROUP_DIM
    qw = DIFF_HEADS * 2 * DIFF_QK_DIM
    assert (batch * lp) % GATHER_ROWS == 0 and TB % HALO == 0 and TB % POOL_SUB == 0

    meta = jnp.broadcast_to(meta_tokens[None].astype(x.dtype), (batch, N_META, d))
    h = jnp.concatenate([meta, x, jnp.zeros((batch, lp - seq_all, d), x.dtype)], axis=1)
    h = h.reshape(batch * lp, d)

    dbias, pbias = _bias_tiles(rel_bias.astype(F32))
    bands = jnp.asarray(_pool_bands(), BF16)
    row = lambda a: a.reshape(1, -1).astype(F32)

    for layer in range(depth):
        u, q, k, v, gates = _inproj(h, row(norm_mix[layer]), w_in[layer].astype(BF16), pw, qw)
        lambda_init = 0.8 - 0.6 * math.exp(-0.3 * layer)
        b_out = _attention(q, k, v, dbias, pbias, row(lambda_q1[layer]), row(lambda_k1[layer]),
                           row(lambda_q2[layer]), row(lambda_k2[layer]), row(subln_gain[layer]),
                           batch=batch, lp=lp, lambda_init=lambda_init)
        h = _merge(h, u, b_out, gates, bands, pool_group_w[layer].astype(BF16), row(pool_scale[layer]),
                   w_pool_up[layer].astype(BF16), w_attn_up[layer].astype(BF16), w_out[layer].astype(BF16),
                   nt=nt)
        j = layer // 2
        if layer % 2 == 0:
            h = _dense_ffn(h, row(norm_ffn[layer]), dense_w_gate[j].astype(BF16),
                           dense_w_up[j].astype(BF16), dense_w_down[j].astype(BF16))
        else:
            h = _moe_layer(h, row(norm_ffn[layer]), moe_router[j].astype(F32), moe_w_gate[j].astype(BF16),
                           moe_w_up[j].astype(BF16), moe_w_down[j].astype(BF16),
                           row(final_norm) if layer == depth - 1 else None)

    out = h if depth % 2 == 0 else _final_norm(h, row(final_norm))
    return out.reshape(batch, lp, d)[:, N_META:seq_all]
```

```python
import functools
import math

import numpy as np
import jax
import jax.numpy as jnp
from jax import lax
from jax.experimental import pallas as pl
from jax.experimental.pallas import tpu as pltpu

F32 = jnp.float32
BF16 = jnp.bfloat16

N_META = 16
POOL_WINDOWS = (2, 4, 8, 16)
POOL_GROUP_DIM = 128
DIFF_HEADS = 4
DIFF_QK_DIM = 64
DIFF_V_DIM = 128
REL_BUCKETS = 32
REL_MAX_DIST = 128
TOP_K = 2
RMS_EPS = 1e-6
SUBLN_EPS = 1e-5

LANES = 128
TB = 768
ATT_UNIT = 128
HALO = 128
POOL_SUB = 256
FF_CHUNK = 512
MOE_TM = 512
GATHER_ROWS = 512
VMEM_LIMIT = 56 * 1024 * 1024
LOG2E = 1.4426950408889634
NEG = -1e30


def _cparams(*sem):
    return pltpu.CompilerParams(dimension_semantics=sem, vmem_limit_bytes=VMEM_LIMIT)


def _rms(x, g, eps):
    ms = jnp.mean(x * x, axis=-1, keepdims=True)
    return x * lax.rsqrt(ms + eps) * g


def _inproj_kernel(h_ref, g_ref, w_ref, u_ref, q_ref, k_ref, v_ref, gate_ref, *, pw, qw):
    hn = _rms(h_ref[...], g_ref[...], RMS_EPS).astype(BF16)

    def proj(lo, hi):
        return jnp.dot(hn, w_ref[:, lo:hi], preferred_element_type=F32)

    u_ref[...] = proj(0, pw).astype(BF16)
    q_ref[...] = (proj(pw, pw + qw) * (DIFF_QK_DIM ** -0.5 * LOG2E)).astype(BF16)
    k_ref[...] = proj(pw + qw, pw + 2 * qw).astype(BF16)
    v = proj(pw + 2 * qw, pw + 3 * qw).astype(BF16)
    ones = jnp.ones((v.shape[0], DIFF_V_DIM), BF16)
    v_ref[...] = jnp.concatenate(
        [blk for h in range(DIFF_HEADS) for blk in (v[:, h * DIFF_V_DIM:(h + 1) * DIFF_V_DIM], ones)], axis=1)
    gate_ref[...] = jax.nn.sigmoid(proj(pw + 3 * qw, w_ref.shape[1])).astype(BF16)


def _inproj(h, g, w, pw, qw):
    rows, d = h.shape
    cols = w.shape[1]
    gw = cols - pw - 3 * qw
    row_blk = lambda c: pl.BlockSpec((TB, c), lambda i: (i, 0))
    return pl.pallas_call(
        functools.partial(_inproj_kernel, pw=pw, qw=qw),
        grid=(rows // TB,),
        in_specs=[row_blk(d),
                  pl.BlockSpec((1, d), lambda i: (0, 0)),
                  pl.BlockSpec((d, cols), lambda i: (0, 0), pipeline_mode=pl.Buffered(1))],
        out_specs=[row_blk(pw), row_blk(qw), row_blk(qw), row_blk(2 * qw), row_blk(gw)],
        out_shape=[jax.ShapeDtypeStruct((rows, c), BF16) for c in (pw, qw, qw, 2 * qw, gw)],
        compiler_params=_cparams("parallel"),
        name="inproj",
    )(h, g, w)


def _bucket_starts():
    max_exact = REL_BUCKETS // 2
    n = np.arange(1, REL_MAX_DIST + 1)
    large = max_exact + (np.log(n.astype(np.float32) / max_exact) / np.float32(math.log(REL_MAX_DIST / max_exact))
                         * (REL_BUCKETS - max_exact)).astype(np.int32)
    bucket = np.where(n < max_exact, n, np.minimum(large, REL_BUCKETS - 1))
    return [0] + [int(n[bucket >= b].min()) for b in range(1, REL_BUCKETS)]


def _bias_tiles_kernel(tab_ref, diag_ref, sub_ref, *, starts):
    h = pl.program_id(0)
    far = tab_ref[REL_BUCKETS - 1, h]
    row = lax.broadcasted_iota(jnp.int32, (ATT_UNIT, ATT_UNIT), 0)
    col = lax.broadcasted_iota(jnp.int32, (ATT_UNIT, ATT_UNIT), 1)
    dist = row - col

    def bias_of(n):
        t = jnp.full(n.shape, (tab_ref[0, h] - far) * LOG2E, F32)
        for b in range(1, REL_BUCKETS):
            t = jnp.where(n >= starts[b], (tab_ref[b, h] - far) * LOG2E, t)
        return t

    diag_ref[0] = jnp.where(dist >= 0, bias_of(dist), NEG)
    sub_ref[0] = bias_of(dist + ATT_UNIT)


def _bias_tiles(rel_bias):
    starts = _bucket_starts()
    assert ATT_UNIT >= starts[-1]
    heads = rel_bias.shape[1]
    tile = pl.BlockSpec((1, ATT_UNIT, ATT_UNIT), lambda h: (h, 0, 0))
    return pl.pallas_call(
        functools.partial(_bias_tiles_kernel, starts=starts),
        grid=(heads,),
        in_specs=[pl.BlockSpec(memory_space=pltpu.SMEM)],
        out_specs=[tile, tile],
        out_shape=[jax.ShapeDtypeStruct((heads, ATT_UNIT, ATT_UNIT), F32)] * 2,
        compiler_params=_cparams("parallel"),
        name="bias_tiles",
    )(rel_bias)


def _attn_kernel(q_ref, k_ref, v_ref, dbias_ref, sbias_ref, lq1_ref, lk1_ref, lq2_ref, lk2_ref,
                 gain_ref, o_ref, m_ref, acc_ref, *, lambda_init):
    i = pl.program_id(2)
    q = q_ref[...]
    lane = lax.broadcasted_iota(jnp.int32, q.shape, 1)
    zero = jnp.zeros_like(q)
    qs = (jnp.where(lane < DIFF_QK_DIM, q, zero), jnp.where(lane >= DIFF_QK_DIM, q, zero))

    m_ref[...] = jnp.full(m_ref.shape, NEG, F32)
    acc_ref[...] = jnp.zeros(acc_ref.shape, F32)

    def step(j, kind, nblk=1):
        start = pl.multiple_of(j * TB, TB)
        kj = k_ref[pl.ds(start, nblk * TB), :]
        vj = v_ref[pl.ds(start, nblk * TB), :]
        for u in range(TB // ATT_UNIT):
            rows = slice(u * ATT_UNIT, (u + 1) * ATT_UNIT)
            n = (u + 1) * ATT_UNIT if kind == "diag" else nblk * TB
            for c in range(2):
                s = lax.dot_general(qs[c][rows], kj[:n], (((1,), (1,)), ((), ())), preferred_element_type=F32)
                cols = [s[:, b * ATT_UNIT:(b + 1) * ATT_UNIT] for b in range(n // ATT_UNIT)]
                if kind == "diag":
                    cols[u] = cols[u] + dbias_ref[0]
                    if u >= 1:
                        cols[u - 1] = cols[u - 1] + sbias_ref[0]
                elif kind == "prev" and u == 0:
                    cols[-1] = cols[-1] + sbias_ref[0]
                m_prev = m_ref[c, rows, :]
                m_cur = jnp.max(functools.reduce(jnp.maximum, cols), axis=-1, keepdims=True)
                m_new = jnp.maximum(m_prev, m_cur)
                alpha = jnp.exp2(m_prev - m_new)
                ps = [jnp.exp2(blk - m_new).astype(BF16) for blk in cols]
                p = ps[0] if len(ps) == 1 else jnp.concatenate(ps, axis=1)
                pv = jnp.dot(p, vj[:n], preferred_element_type=F32)
                acc_ref[c, rows, :] = jnp.concatenate([alpha, alpha], axis=1) * acc_ref[c, rows, :] + pv
                m_ref[c, rows, :] = m_new

    n_far = jnp.maximum(i - 1, 0)

    def far_pair(t, carry):
        step(2 * t, "far", nblk=2)
        return carry

    lax.fori_loop(0, n_far // 2, far_pair, 0)

    @pl.when(n_far % 2 == 1)
    def _():
        step(n_far - 1, "far")

    @pl.when(i >= 1)
    def _():
        step(i - 1, "prev")

    step(i, "diag")

    lam = (jnp.exp(jnp.sum(lq1_ref[...] * lk1_ref[...], axis=-1, keepdims=True))
           - jnp.exp(jnp.sum(lq2_ref[...] * lk2_ref[...], axis=-1, keepdims=True))
           + lambda_init)
    o = (acc_ref[0, :, :DIFF_V_DIM] / acc_ref[0, :, DIFF_V_DIM:]
         - lam * (acc_ref[1, :, :DIFF_V_DIM] / acc_ref[1, :, DIFF_V_DIM:]))
    o_ref[...] = (_rms(o, gain_ref[...], SUBLN_EPS) * (1.0 - lambda_init)).astype(BF16)


def _attention(q, k, v, dbias, sbias, lq1, lk1, lq2, lk2, gain, *, batch, lp, lambda_init):
    rows, width = q.shape
    heads = width // (2 * DIFF_QK_DIM)
    nq = lp // TB
    qo_spec = pl.BlockSpec((TB, LANES), lambda b, h, i: (b * nq + i, h))
    k_spec = pl.BlockSpec((lp, 2 * DIFF_QK_DIM), lambda b, h, i: (b, h))
    v_spec = pl.BlockSpec((lp, 2 * DIFF_V_DIM), lambda b, h, i: (b, h))
    bias_spec = pl.BlockSpec((1, ATT_UNIT, ATT_UNIT), lambda b, h, i: (h, 0, 0))
    vec_spec = lambda n: pl.BlockSpec((1, n), lambda b, h, i: (0, 0))
    return pl.pallas_call(
        functools.partial(_attn_kernel, lambda_init=lambda_init),
        grid=(batch, heads, nq),
        in_specs=[qo_spec, k_spec, v_spec, bias_spec, bias_spec,
                  vec_spec(DIFF_QK_DIM), vec_spec(DIFF_QK_DIM), vec_spec(DIFF_QK_DIM),
                  vec_spec(DIFF_QK_DIM), vec_spec(DIFF_V_DIM)],
        out_specs=qo_spec,
        out_shape=jax.ShapeDtypeStruct((rows, heads * DIFF_V_DIM), BF16),
        scratch_shapes=[pltpu.VMEM((2, TB, LANES), F32), pltpu.VMEM((2, TB, 2 * DIFF_V_DIM), F32)],
        compiler_params=_cparams("parallel", "parallel", "arbitrary"),
        name="diff_attn",
    )(q, k, v, dbias, sbias, lq1, lk1, lq2, lk2, gain)


def _pool_bands():
    r = np.arange(POOL_SUB)[:, None] + HALO
    c = np.arange(POOL_SUB + HALO)[None, :]
    d = r - c
    return np.stack([((d >= 0) & (d < w)).astype(np.float32) for w in POOL_WINDOWS])


def _merge_kernel(h_ref, u_ref, halo_ref, b_ref, gate_ref, band_ref, gw_ref, scale_ref,
                  wpu_ref, wau_ref, wo_ref, o_ref, *, nt):
    t = pl.program_id(0) % nt
    u = u_ref[...]
    halo = halo_ref[...]
    halo = jnp.where(t > 0, halo, jnp.zeros_like(halo))
    xcat = jnp.concatenate([halo, u], axis=0)
    uf = u.astype(F32)
    pos = t * TB + lax.broadcasted_iota(jnp.int32, (TB, 1), 0)
    d_model = o_ref.shape[1]

    ys = []
    for g, w in enumerate(POOL_WINDOWS):
        lo, hi = g * POOL_GROUP_DIM, (g + 1) * POOL_GROUP_DIM
        cnt = jnp.minimum(pos + 1, w).astype(F32)
        sums = [jnp.dot(band_ref[g], xcat[s * POOL_SUB:s * POOL_SUB + POOL_SUB + HALO, lo:hi],
                        preferred_element_type=F32) for s in range(TB // POOL_SUB)]
        mixed = (jnp.concatenate(sums, axis=0) / cnt - uf[:, lo:hi]).astype(BF16)
        ys.append(jnp.dot(mixed, gw_ref[g], preferred_element_type=F32))
    a_out = (jnp.concatenate(ys, axis=1) * scale_ref[...]).astype(BF16)
    pa = jnp.dot(a_out, wpu_ref[...], preferred_element_type=F32)
    pb = jnp.dot(b_ref[...], wau_ref[...], preferred_element_type=F32)
    gates = gate_ref[...]
    merged = (gates[:, :d_model].astype(F32) * pa + gates[:, d_model:].astype(F32) * pb).astype(BF16)
    o_ref[...] = h_ref[...] + jnp.dot(merged, wo_ref[...], preferred_element_type=F32)


def _merge(h, u, b_out, gates, bands, gw, scale, wpu, wau, wo, *, nt):
    rows, d = h.shape
    pw = u.shape[1]
    halo_per_tile = TB // HALO
    row_blk = lambda c: pl.BlockSpec((TB, c), lambda i: (i, 0))
    full = lambda a: pl.BlockSpec(a.shape, lambda i: (0,) * a.ndim)
    return pl.pallas_call(
        functools.partial(_merge_kernel, nt=nt),
        grid=(rows // TB,),
        in_specs=[row_blk(d), row_blk(pw),
                  pl.BlockSpec((HALO, pw), lambda i: (jnp.maximum(i * halo_per_tile - 1, 0), 0)),
                  row_blk(b_out.shape[1]), row_blk(gates.shape[1]),
                  full(bands), full(gw), full(scale), full(wpu), full(wau), full(wo)],
        out_specs=row_blk(d),
        out_shape=jax.ShapeDtypeStruct((rows, d), F32),
        compiler_params=_cparams("parallel"),
        name="pool_merge",
    )(h, u, u, b_out, gates, bands, gw, scale, wpu, wau, wo)


def _swiglu_chunk(x, wg, wu, wd):
    g = jnp.dot(x, wg, preferred_element_type=F32)
    u = jnp.dot(x, wu, preferred_element_type=F32)
    a = (g * jax.nn.sigmoid(g) * u).astype(BF16)
    return jnp.dot(a, wd, preferred_element_type=F32)


def _dense_ffn_kernel(h_ref, g_ref, wg_ref, wu_ref, wd_ref, o_ref, hn_ref, acc_ref):
    f = pl.program_id(1)

    @pl.when(f == 0)
    def _():
        hn_ref[...] = _rms(h_ref[...], g_ref[...], RMS_EPS).astype(BF16)
        acc_ref[...] = jnp.zeros(acc_ref.shape, F32)

    acc_ref[...] += _swiglu_chunk(hn_ref[...], wg_ref[...], wu_ref[...], wd_ref[...])

    @pl.when(f == pl.num_programs(1) - 1)
    def _():
        o_ref[...] = h_ref[...] + acc_ref[...]


def _dense_ffn(h, g, wg, wu, wd):
    rows, d = h.shape
    dff = wg.shape[1]
    return pl.pallas_call(
        _dense_ffn_kernel,
        grid=(rows // TB, dff // FF_CHUNK),
        in_specs=[pl.BlockSpec((TB, d), lambda i, f: (i, 0)),
                  pl.BlockSpec((1, d), lambda i, f: (0, 0)),
                  pl.BlockSpec((d, FF_CHUNK), lambda i, f: (0, f)),
                  pl.BlockSpec((d, FF_CHUNK), lambda i, f: (0, f)),
                  pl.BlockSpec((FF_CHUNK, d), lambda i, f: (f, 0))],
        out_specs=pl.BlockSpec((TB, d), lambda i, f: (i, 0)),
        out_shape=jax.ShapeDtypeStruct((rows, d), F32),
        scratch_shapes=[pltpu.VMEM((TB, d), BF16), pltpu.VMEM((TB, d), F32)],
        compiler_params=_cparams("parallel", "arbitrary"),
        name="dense_ffn",
    )(h, g, wg, wu, wd)


def _split_bf16(x):
    hi = x.astype(BF16)
    return hi, (x - hi.astype(F32)).astype(BF16)


def _router_kernel(h_ref, g_ref, r_ref, hn_ref, idx_ref, wt_ref, *, n_experts):
    hn = _rms(h_ref[...], g_ref[...], RMS_EPS)
    hn_ref[...] = hn
    x_hi, x_lo = _split_bf16(hn)
    r_hi, r_lo = _split_bf16(r_ref[...])
    logits = (jnp.dot(x_hi, r_hi, preferred_element_type=F32)
              + jnp.dot(x_lo, r_hi, preferred_element_type=F32)
              + jnp.dot(x_hi, r_lo, preferred_element_type=F32))
    lane = lax.broadcasted_iota(jnp.int32, logits.shape, 1)
    ninf = jnp.float32(-jnp.inf)
    lg = jnp.where(lane < n_experts, logits, ninf)
    v1 = jnp.max(lg, axis=-1, keepdims=True)
    i1 = jnp.min(jnp.where(lg == v1, lane, LANES), axis=-1, keepdims=True)
    lg2 = jnp.where(lane == i1, ninf, lg)
    v2 = jnp.max(lg2, axis=-1, keepdims=True)
    i2 = jnp.min(jnp.where(lg2 == v2, lane, LANES), axis=-1, keepdims=True)
    e = jnp.exp(v2 - v1)
    w1 = 1.0 / (1.0 + e)
    w2 = e / (1.0 + e)
    idx_ref[...] = jnp.where(lane == 0, i1, jnp.where(lane == 1, i2, 0))
    wt_ref[...] = jnp.where(lane == 0, w1, jnp.where(lane == 1, w2, 0.0))


def _router(h, g, router_padded, n_experts):
    rows, d = h.shape
    row_blk = lambda c: pl.BlockSpec((TB, c), lambda i: (i, 0))
    return pl.pallas_call(
        functools.partial(_router_kernel, n_experts=n_experts),
        grid=(rows // TB,),
        in_specs=[row_blk(d), pl.BlockSpec((1, d), lambda i: (0, 0)),
                  pl.BlockSpec((d, LANES), lambda i: (0, 0))],
        out_specs=[row_blk(d), row_blk(LANES), row_blk(LANES)],
        out_shape=[jax.ShapeDtypeStruct((rows, d), F32),
                   jax.ShapeDtypeStruct((rows, LANES), jnp.int32),
                   jax.ShapeDtypeStruct((rows, LANES), F32)],
        compiler_params=_cparams("parallel"),
        name="moe_router",
    )(h, g, router_padded)


def _row_copy(src_ref, dst_ref, src_row, dst_row, sem):
    return pltpu.make_async_copy(src_ref.at[pl.ds(src_row, 1), :], dst_ref.at[pl.ds(dst_row, 1), :], sem)


def _for_rows(n_rows, fn):
    def body(r, carry):
        fn(r)
        return carry

    lax.fori_loop(0, n_rows, body, 0, unroll=8)


def _idx_blocks(idx):
    return idx.reshape(idx.shape[0] // GATHER_ROWS, 1, GATHER_ROWS)


_IDX_SPEC = pl.BlockSpec((1, 1, GATHER_ROWS), lambda i: (i, 0, 0), memory_space=pltpu.SMEM)


def _dispatch_kernel(p1_ref, p2_ref, hn_ref, xs_init_hbm, xs_hbm, sem):
    del xs_init_hbm
    n = hn_ref.shape[0]

    def start(r):
        _row_copy(hn_ref, xs_hbm, r, p1_ref[0, 0, r], sem.at[0]).start()
        _row_copy(hn_ref, xs_hbm, r, p2_ref[0, 0, r], sem.at[1]).start()

    def wait(r):
        _row_copy(hn_ref, xs_hbm, r, 0, sem.at[0]).wait()
        _row_copy(hn_ref, xs_hbm, r, 0, sem.at[1]).wait()

    _for_rows(n, start)
    _for_rows(n, wait)


def _dispatch(hn, pos1, pos2, sorted_rows):
    rows, d = hn.shape
    return pl.pallas_call(
        _dispatch_kernel,
        grid=(rows // GATHER_ROWS,),
        in_specs=[_IDX_SPEC, _IDX_SPEC, pl.BlockSpec((GATHER_ROWS, d), lambda i: (i, 0)),
                  pl.BlockSpec(memory_space=pl.ANY)],
        out_specs=pl.BlockSpec(memory_space=pl.ANY),
        out_shape=jax.ShapeDtypeStruct((sorted_rows, d), hn.dtype),
        scratch_shapes=[pltpu.SemaphoreType.DMA((2,))],
        input_output_aliases={3: 0},
        compiler_params=_cparams("arbitrary"),
        name="moe_dispatch",
    )(_idx_blocks(pos1), _idx_blocks(pos2), hn, jnp.zeros((sorted_rows, d), hn.dtype))


def _moe_ffn_kernel(te_ref, tv_ref, x_ref, wg_ref, wu_ref, wd_ref, o_ref, acc_ref):
    i, f = pl.program_id(0), pl.program_id(1)

    @pl.when(f == 0)
    def _():
        acc_ref[...] = jnp.zeros(acc_ref.shape, F32)

    @pl.when(tv_ref[i] > 0)
    def _():
        acc_ref[...] += _swiglu_chunk(x_ref[...].astype(BF16), wg_ref[0], wu_ref[0], wd_ref[0])

    @pl.when(f == pl.num_programs(1) - 1)
    def _():
        o_ref[...] = acc_ref[...]


def _moe_ffn(tile_expert, tile_valid, xs, wg, wu, wd):
    rows, d = xs.shape
    dff = wg.shape[2]
    nf = dff // FF_CHUNK
    fidx = lambda f, i, tv: jnp.where(tv[i] > 0, f, nf - 1)
    grid_spec = pltpu.PrefetchScalarGridSpec(
        num_scalar_prefetch=2,
        grid=(rows // MOE_TM, nf),
        in_specs=[pl.BlockSpec((MOE_TM, d), lambda i, f, te, tv: (i, 0)),
                  pl.BlockSpec((1, d, FF_CHUNK), lambda i, f, te, tv: (te[i], 0, fidx(f, i, tv))),
                  pl.BlockSpec((1, d, FF_CHUNK), lambda i, f, te, tv: (te[i], 0, fidx(f, i, tv))),
                  pl.BlockSpec((1, FF_CHUNK, d), lambda i, f, te, tv: (te[i], fidx(f, i, tv), 0))],
        out_specs=pl.BlockSpec((MOE_TM, d), lambda i, f, te, tv: (i, 0)),
        scratch_shapes=[pltpu.VMEM((MOE_TM, d), F32)],
    )
    return pl.pallas_call(
        _moe_ffn_kernel,
        grid_spec=grid_spec,
        out_shape=jax.ShapeDtypeStruct((rows, d), F32),
        compiler_params=_cparams("arbitrary", "arbitrary"),
        name="moe_ffn",
    )(tile_expert, tile_valid, xs, wg, wu, wd)


def _combine_kernel(p1_ref, p2_ref, h_ref, wt_ref, ys_hbm, g_ref, o_ref, y1_ref, y2_ref, sem, *, final):
    n = o_ref.shape[0]

    def start(r):
        _row_copy(ys_hbm, y1_ref, p1_ref[0, 0, r], r, sem.at[0]).start()
        _row_copy(ys_hbm, y2_ref, p2_ref[0, 0, r], r, sem.at[1]).start()

    def wait(r):
        _row_copy(ys_hbm, y1_ref, 0, r, sem.at[0]).wait()
        _row_copy(ys_hbm, y2_ref, 0, r, sem.at[1]).wait()

    _for_rows(n, start)
    _for_rows(n, wait)
    wt = wt_ref[...]
    out = h_ref[...] + (wt[:, 0:1] * y1_ref[...] + wt[:, 1:2] * y2_ref[...])
    o_ref[...] = _rms(out, g_ref[...], RMS_EPS) if final else out


def _combine(h, wts, ys, pos1, pos2, final_gain):
    rows, d = h.shape
    final = final_gain is not None
    gain = final_gain if final else jnp.ones((1, d), F32)
    row_blk = lambda c: pl.BlockSpec((GATHER_ROWS, c), lambda i: (i, 0))
    return pl.pallas_call(
        functools.partial(_combine_kernel, final=final),
        grid=(rows // GATHER_ROWS,),
        in_specs=[_IDX_SPEC, _IDX_SPEC, row_blk(d), row_blk(LANES),
                  pl.BlockSpec(memory_space=pl.ANY), pl.BlockSpec((1, d), lambda i: (0, 0))],
        out_specs=row_blk(d),
        out_shape=jax.ShapeDtypeStruct((rows, d), F32),
        scratch_shapes=[pltpu.VMEM((GATHER_ROWS, d), F32), pltpu.VMEM((GATHER_ROWS, d), F32),
                        pltpu.SemaphoreType.DMA((2,))],
        compiler_params=_cparams("arbitrary"),
        name="moe_combine",
    )(_idx_blocks(pos1), _idx_blocks(pos2), h, wts, ys, gain)


def _moe_layer(h, g, router, wg, wu, wd, final_gain):
    rows, d = h.shape
    n_experts = router.shape[1]
    router_padded = jnp.pad(router, ((0, 0), (0, LANES - n_experts)))
    hn, idx, wts = _router(h, g, router_padded, n_experts)

    n_assign = rows * TOP_K
    e_flat = idx[:, :TOP_K].reshape(n_assign)
    onehot = (e_flat[:, None] == jnp.arange(n_experts, dtype=jnp.int32)[None, :]).astype(jnp.int32)
    csum = jnp.cumsum(onehot, axis=0)
    rank = jnp.take_along_axis(csum, e_flat[:, None], axis=1)[:, 0] - 1
    counts = csum[-1]
    padded = ((counts + MOE_TM - 1) // MOE_TM) * MOE_TM
    ends = jnp.cumsum(padded)
    pos = (ends - padded)[e_flat] + rank
    n_tiles = n_assign // MOE_TM + n_experts
    tile_start = jnp.arange(n_tiles, dtype=jnp.int32) * MOE_TM
    tile_expert = jnp.minimum(jnp.searchsorted(ends, tile_start, side="right"), n_experts - 1).astype(jnp.int32)
    tile_valid = (tile_start < ends[-1]).astype(jnp.int32)

    pos2d = pos.reshape(rows, TOP_K)
    xs = _dispatch(hn, pos2d[:, 0], pos2d[:, 1], n_tiles * MOE_TM)
    ys = _moe_ffn(tile_expert, tile_valid, xs, wg, wu, wd)
    return _combine(h, wts, ys, pos2d[:, 0], pos2d[:, 1], final_gain)


def _final_norm_kernel(h_ref, g_ref, o_ref):
    o_ref[...] = _rms(h_ref[...], g_ref[...], RMS_EPS)


def _final_norm(h, g):
    rows, d = h.shape
    return pl.pallas_call(
        _final_norm_kernel,
        grid=(rows // TB,),
        in_specs=[pl.BlockSpec((TB, d), lambda i: (i, 0)), pl.BlockSpec((1, d), lambda i: (0, 0))],
        out_specs=pl.BlockSpec((TB, d), lambda i: (i, 0)),
        out_shape=jax.ShapeDtypeStruct((rows, d), F32),
        compiler_params=_cparams("parallel"),
        name="final_norm",
    )(h, g)


def kernel(x, meta_tokens, rel_bias, norm_mix, w_in, pool_group_w, pool_scale, lambda_q1, lambda_k1, lambda_q2, lambda_k2, subln_gain, w_pool_up, w_attn_up, w_out, norm_ffn, dense_w_gate, dense_w_up, dense_w_down, moe_router, moe_w_gate, moe_w_up, moe_w_down, final_norm):
    batch, seq, d = x.shape
    depth = w_in.shape[0]
    seq_all = N_META + seq
    lp = -(-seq_all // TB) * TB
    nt = lp // TB
    pw = len(POOL_WINDOWS) * POOL_GROUP_DIM
    qw = DIFF_HEADS * 2 * DIFF_QK_DIM
    assert (batch * lp) % GATHER_ROWS == 0 and TB % HALO == 0 and TB % POOL_SUB == 0

    meta = jnp.broadcast_to(meta_tokens[None].astype(x.dtype), (batch, N_META, d))
    h = jnp.concatenate([meta, x, jnp.zeros((batch, lp - seq_all, d), x.dtype)], axis=1)
    h = h.reshape(batch * lp, d)

    dbias, sbias = _bias_tiles(rel_bias.astype(F32))
    bands = jnp.asarray(_pool_bands(), BF16)
    row = lambda a: a.reshape(1, -1).astype(F32)

    for layer in range(depth):
        u, q, k, v, gates = _inproj(h, row(norm_mix[layer]), w_in[layer].astype(BF16), pw, qw)
        lambda_init = 0.8 - 0.6 * math.exp(-0.3 * layer)
        b_out = _attention(q, k, v, dbias, sbias, row(lambda_q1[layer]), row(lambda_k1[layer]),
                           row(lambda_q2[layer]), row(lambda_k2[layer]), row(subln_gain[layer]),
                           batch=batch, lp=lp, lambda_init=lambda_init)
        h = _merge(h, u, b_out, gates, bands, pool_group_w[layer].astype(BF16), row(pool_scale[layer]),
                   w_pool_up[layer].astype(BF16), w_attn_up[layer].astype(BF16), w_out[layer].astype(BF16),
                   nt=nt)
        j = layer // 2
        if layer % 2 == 0:
            h = _dense_ffn(h, row(norm_ffn[layer]), dense_w_gate[j].astype(BF16),
                           dense_w_up[j].astype(BF16), dense_w_down[j].astype(BF16))
        else:
            h = _moe_layer(h, row(norm_ffn[layer]), moe_router[j].astype(F32), moe_w_gate[j].astype(BF16),
                           moe_w_up[j].astype(BF16), moe_w_down[j].astype(BF16),
                           row(final_norm) if layer == depth - 1 else None)

    out = h if depth % 2 == 0 else _final_norm(h, row(final_norm))
    return out.reshape(batch, lp, d)[:, N_META:seq_all]
```

```python
import functools
import math

import numpy as np
import jax
import jax.numpy as jnp
from jax import lax
from jax.experimental import pallas as pl
from jax.experimental.pallas import tpu as pltpu

F32 = jnp.float32
BF16 = jnp.bfloat16

N_META = 16
POOL_WINDOWS = (2, 4, 8, 16)
POOL_GROUP_DIM = 128
DIFF_HEADS = 4
DIFF_QK_DIM = 64
DIFF_V_DIM = 128
REL_BUCKETS = 32
REL_MAX_DIST = 128
TOP_K = 2
RMS_EPS = 1e-6
SUBLN_EPS = 1e-5

LANES = 128
TB = 768
ATT_UNIT = 128
HALO = 128
POOL_SUB = 256
FF_CHUNK = 512
MOE_TM = 512
GATHER_ROWS = 512
VMEM_LIMIT = 56 * 1024 * 1024
LOG2E = 1.4426950408889634
NEG = -1e30


def _cparams(*sem):
    return pltpu.CompilerParams(dimension_semantics=sem, vmem_limit_bytes=VMEM_LIMIT)


def _rms(x, g, eps):
    ms = jnp.mean(x * x, axis=-1, keepdims=True)
    return x * lax.rsqrt(ms + eps) * g


def _inproj_kernel(h_ref, g_ref, w_ref, u_ref, q_ref, k_ref, v_ref, gate_ref, *, pw, qw):
    hn = _rms(h_ref[...], g_ref[...], RMS_EPS).astype(BF16)

    def proj(lo, hi):
        return jnp.dot(hn, w_ref[:, lo:hi], preferred_element_type=F32)

    u_ref[...] = proj(0, pw).astype(BF16)
    q_ref[...] = (proj(pw, pw + qw) * (DIFF_QK_DIM ** -0.5 * LOG2E)).astype(BF16)
    k_ref[...] = proj(pw + qw, pw + 2 * qw).astype(BF16)
    v = proj(pw + 2 * qw, pw + 3 * qw).astype(BF16)
    ones = jnp.ones((v.shape[0], DIFF_V_DIM), BF16)
    v_ref[...] = jnp.concatenate(
        [blk for h in range(DIFF_HEADS) for blk in (v[:, h * DIFF_V_DIM:(h + 1) * DIFF_V_DIM], ones)], axis=1)
    gate_ref[...] = jax.nn.sigmoid(proj(pw + 3 * qw, w_ref.shape[1])).astype(BF16)


def _inproj(h, g, w, pw, qw):
    rows, d = h.shape
    cols = w.shape[1]
    gw = cols - pw - 3 * qw
    row_blk = lambda c: pl.BlockSpec((TB, c), lambda i: (i, 0))
    return pl.pallas_call(
        functools.partial(_inproj_kernel, pw=pw, qw=qw),
        grid=(rows // TB,),
        in_specs=[row_blk(d),
                  pl.BlockSpec((1, d), lambda i: (0, 0)),
                  pl.BlockSpec((d, cols), lambda i: (0, 0), pipeline_mode=pl.Buffered(1))],
        out_specs=[row_blk(pw), row_blk(qw), row_blk(qw), row_blk(2 * qw), row_blk(gw)],
        out_shape=[jax.ShapeDtypeStruct((rows, c), BF16) for c in (pw, qw, qw, 2 * qw, gw)],
        compiler_params=_cparams("parallel"),
        name="inproj",
    )(h, g, w)


def _bucket_starts():
    max_exact = REL_BUCKETS // 2
    n = np.arange(1, REL_MAX_DIST + 1)
    large = max_exact + (np.log(n.astype(np.float32) / max_exact) / np.float32(math.log(REL_MAX_DIST / max_exact))
                         * (REL_BUCKETS - max_exact)).astype(np.int32)
    bucket = np.where(n < max_exact, n, np.minimum(large, REL_BUCKETS - 1))
    return [0] + [int(n[bucket >= b].min()) for b in range(1, REL_BUCKETS)]


def _bias_tiles_kernel(tab_ref, diag_ref, sub_ref, *, starts):
    h = pl.program_id(0)
    far = tab_ref[REL_BUCKETS - 1, h]
    row = lax.broadcasted_iota(jnp.int32, (ATT_UNIT, ATT_UNIT), 0)
    col = lax.broadcasted_iota(jnp.int32, (ATT_UNIT, ATT_UNIT), 1)
    dist = row - col

    def bias_of(n):
        t = jnp.full(n.shape, (tab_ref[0, h] - far) * LOG2E, F32)
        for b in range(1, REL_BUCKETS):
            t = jnp.where(n >= starts[b], (tab_ref[b, h] - far) * LOG2E, t)
        return t

    diag_ref[0] = jnp.where(dist >= 0, bias_of(dist), NEG)
    sub_ref[0] = bias_of(dist + ATT_UNIT)


def _bias_tiles(rel_bias):
    starts = _bucket_starts()
    assert ATT_UNIT >= starts[-1]
    heads = rel_bias.shape[1]
    tile = pl.BlockSpec((1, ATT_UNIT, ATT_UNIT), lambda h: (h, 0, 0))
    return pl.pallas_call(
        functools.partial(_bias_tiles_kernel, starts=starts),
        grid=(heads,),
        in_specs=[pl.BlockSpec(memory_space=pltpu.SMEM)],
        out_specs=[tile, tile],
        out_shape=[jax.ShapeDtypeStruct((heads, ATT_UNIT, ATT_UNIT), F32)] * 2,
        compiler_params=_cparams("parallel"),
        name="bias_tiles",
    )(rel_bias)


def _attn_kernel(q_ref, k_ref, v_ref, dbias_ref, sbias_ref, lq1_ref, lk1_ref, lq2_ref, lk2_ref,
                 gain_ref, o_ref, m_ref, acc_ref, *, lambda_init):
    i = pl.program_id(2)
    q = q_ref[...]
    lane = lax.broadcasted_iota(jnp.int32, q.shape, 1)
    zero = jnp.zeros_like(q)
    qs = (jnp.where(lane < DIFF_QK_DIM, q, zero), jnp.where(lane >= DIFF_QK_DIM, q, zero))

    m_ref[...] = jnp.full(m_ref.shape, NEG, F32)
    acc_ref[...] = jnp.zeros(acc_ref.shape, F32)

    def step(j, kind, nblk=1):
        start = pl.multiple_of(j * TB, TB)
        kj = k_ref[pl.ds(start, nblk * TB), :]
        vj = v_ref[pl.ds(start, nblk * TB), :]
        for u in range(TB // ATT_UNIT):
            rows = slice(u * ATT_UNIT, (u + 1) * ATT_UNIT)
            n = (u + 1) * ATT_UNIT if kind == "diag" else nblk * TB
            for c in range(2):
                s = lax.dot_general(qs[c][rows], kj[:n], (((1,), (1,)), ((), ())), preferred_element_type=F32)
                cols = [s[:, b * ATT_UNIT:(b + 1) * ATT_UNIT] for b in range(n // ATT_UNIT)]
                if kind == "diag":
                    cols[u] = cols[u] + dbias_ref[0]
                    if u >= 1:
                        cols[u - 1] = cols[u - 1] + sbias_ref[0]
                elif kind == "prev" and u == 0:
                    cols[-1] = cols[-1] + sbias_ref[0]
                m_prev = m_ref[c, rows, :]
                m_cur = jnp.max(functools.reduce(jnp.maximum, cols), axis=-1, keepdims=True)
                m_new = jnp.maximum(m_prev, m_cur)
                alpha = jnp.exp2(m_prev - m_new)
                ps = [jnp.exp2(blk - m_new).astype(BF16) for blk in cols]
                p = ps[0] if len(ps) == 1 else jnp.concatenate(ps, axis=1)
                pv = jnp.dot(p, vj[:n], preferred_element_type=F32)
                acc_ref[c, rows, :] = jnp.concatenate([alpha, alpha], axis=1) * acc_ref[c, rows, :] + pv
                m_ref[c, rows, :] = m_new

    n_far = jnp.maximum(i - 1, 0)

    def far_pair(t, carry):
        step(2 * t, "far", nblk=2)
        return carry

    lax.fori_loop(0, n_far // 2, far_pair, 0)

    @pl.when(n_far % 2 == 1)
    def _():
        step(n_far - 1, "far")

    @pl.when(i >= 1)
    def _():
        step(i - 1, "prev")

    step(i, "diag")

    lam = (jnp.exp(jnp.sum(lq1_ref[...] * lk1_ref[...], axis=-1, keepdims=True))
           - jnp.exp(jnp.sum(lq2_ref[...] * lk2_ref[...], axis=-1, keepdims=True))
           + lambda_init)
    o = (acc_ref[0, :, :DIFF_V_DIM] / acc_ref[0, :, DIFF_V_DIM:]
         - lam * (acc_ref[1, :, :DIFF_V_DIM] / acc_ref[1, :, DIFF_V_DIM:]))
    o_ref[...] = (_rms(o, gain_ref[...], SUBLN_EPS) * (1.0 - lambda_init)).astype(BF16)


def _attention(q, k, v, dbias, sbias, lq1, lk1, lq2, lk2, gain, *, batch, lp, lambda_init):
    rows, width = q.shape
    heads = width // (2 * DIFF_QK_DIM)
    nq = lp // TB
    qo_spec = pl.BlockSpec((TB, LANES), lambda b, h, i: (b * nq + i, h))
    k_spec = pl.BlockSpec((lp, 2 * DIFF_QK_DIM), lambda b, h, i: (b, h))
    v_spec = pl.BlockSpec((lp, 2 * DIFF_V_DIM), lambda b, h, i: (b, h))
    bias_spec = pl.BlockSpec((1, ATT_UNIT, ATT_UNIT), lambda b, h, i: (h, 0, 0))
    vec_spec = lambda n: pl.BlockSpec((1, n), lambda b, h, i: (0, 0))
    return pl.pallas_call(
        functools.partial(_attn_kernel, lambda_init=lambda_init),
        grid=(batch, heads, nq),
        in_specs=[qo_spec, k_spec, v_spec, bias_spec, bias_spec,
                  vec_spec(DIFF_QK_DIM), vec_spec(DIFF_QK_DIM), vec_spec(DIFF_QK_DIM),
                  vec_spec(DIFF_QK_DIM), vec_spec(DIFF_V_DIM)],
        out_specs=qo_spec,
        out_shape=jax.ShapeDtypeStruct((rows, heads * DIFF_V_DIM), BF16),
        scratch_shapes=[pltpu.VMEM((2, TB, LANES), F32), pltpu.VMEM((2, TB, 2 * DIFF_V_DIM), F32)],
        compiler_params=_cparams("parallel", "parallel", "arbitrary"),
        name="diff_attn",
    )(q, k, v, dbias, sbias, lq1, lk1, lq2, lk2, gain)


def _pool_bands():
    r = np.arange(POOL_SUB)[:, None] + HALO
    c = np.arange(POOL_SUB + HALO)[None, :]
    d = r - c
    return np.stack([((d >= 0) & (d < w)).astype(np.float32) for w in POOL_WINDOWS])


def _merge_kernel(h_ref, u_ref, halo_ref, b_ref, gate_ref, band_ref, gw_ref, scale_ref,
                  wpu_ref, wau_ref, wo_ref, o_ref, *, nt):
    t = pl.program_id(0) % nt
    u = u_ref[...]
    halo = halo_ref[...]
    halo = jnp.where(t > 0, halo, jnp.zeros_like(halo))
    xcat = jnp.concatenate([halo, u], axis=0)
    uf = u.astype(F32)
    pos = t * TB + lax.broadcasted_iota(jnp.int32, (TB, 1), 0)
    d_model = o_ref.shape[1]

    ys = []
    for g, w in enumerate(POOL_WINDOWS):
        lo, hi = g * POOL_GROUP_DIM, (g + 1) * POOL_GROUP_DIM
        cnt = jnp.minimum(pos + 1, w).astype(F32)
        sums = [jnp.dot(band_ref[g], xcat[s * POOL_SUB:s * POOL_SUB + POOL_SUB + HALO, lo:hi],
                        preferred_element_type=F32) for s in range(TB // POOL_SUB)]
        mixed = (jnp.concatenate(sums, axis=0) / cnt - uf[:, lo:hi]).astype(BF16)
        ys.append(jnp.dot(mixed, gw_ref[g], preferred_element_type=F32))
    a_out = (jnp.concatenate(ys, axis=1) * scale_ref[...]).astype(BF16)
    pa = jnp.dot(a_out, wpu_ref[...], preferred_element_type=F32)
    pb = jnp.dot(b_ref[...], wau_ref[...], preferred_element_type=F32)
    gates = gate_ref[...]
    merged = (gates[:, :d_model].astype(F32) * pa + gates[:, d_model:].astype(F32) * pb).astype(BF16)
    o_ref[...] = h_ref[...] + jnp.dot(merged, wo_ref[...], preferred_element_type=F32)


def _merge(h, u, b_out, gates, bands, gw, scale, wpu, wau, wo, *, nt):
    rows, d = h.shape
    pw = u.shape[1]
    halo_per_tile = TB // HALO
    row_blk = lambda c: pl.BlockSpec((TB, c), lambda i: (i, 0))
    full = lambda a: pl.BlockSpec(a.shape, lambda i: (0,) * a.ndim)
    return pl.pallas_call(
        functools.partial(_merge_kernel, nt=nt),
        grid=(rows // TB,),
        in_specs=[row_blk(d), row_blk(pw),
                  pl.BlockSpec((HALO, pw), lambda i: (jnp.maximum(i * halo_per_tile - 1, 0), 0)),
                  row_blk(b_out.shape[1]), row_blk(gates.shape[1]),
                  full(bands), full(gw), full(scale), full(wpu), full(wau), full(wo)],
        out_specs=row_blk(d),
        out_shape=jax.ShapeDtypeStruct((rows, d), F32),
        compiler_params=_cparams("parallel"),
        name="pool_merge",
    )(h, u, u, b_out, gates, bands, gw, scale, wpu, wau, wo)


def _swiglu(x, wg_ref, wu_ref, wd_ref, a_ref, lead=()):
    dff = a_ref.shape[1]
    for lo in range(0, dff, FF_CHUNK):
        cols = lead + (slice(None), slice(lo, lo + FF_CHUNK))
        g = jnp.dot(x, wg_ref[cols], preferred_element_type=F32)
        u = jnp.dot(x, wu_ref[cols], preferred_element_type=F32)
        a_ref[:, lo:lo + FF_CHUNK] = (g * jax.nn.sigmoid(g) * u).astype(BF16)
    return jnp.dot(a_ref[...], wd_ref[lead + (slice(None), slice(None))], preferred_element_type=F32)


def _dense_ffn_kernel(h_ref, g_ref, wg_ref, wu_ref, wd_ref, o_ref, a_ref):
    h = h_ref[...]
    hn = _rms(h, g_ref[...], RMS_EPS).astype(BF16)
    o_ref[...] = h + _swiglu(hn, wg_ref, wu_ref, wd_ref, a_ref)


def _dense_ffn(h, g, wg, wu, wd):
    rows, d = h.shape
    dff = wg.shape[1]
    resident = lambda a: pl.BlockSpec(a.shape, lambda i: (0, 0), pipeline_mode=pl.Buffered(1))
    return pl.pallas_call(
        _dense_ffn_kernel,
        grid=(rows // TB,),
        in_specs=[pl.BlockSpec((TB, d), lambda i: (i, 0)),
                  pl.BlockSpec((1, d), lambda i: (0, 0)),
                  resident(wg), resident(wu), resident(wd)],
        out_specs=pl.BlockSpec((TB, d), lambda i: (i, 0)),
        out_shape=jax.ShapeDtypeStruct((rows, d), F32),
        scratch_shapes=[pltpu.VMEM((TB, dff), BF16)],
        compiler_params=_cparams("parallel"),
        name="dense_ffn",
    )(h, g, wg, wu, wd)


def _split_bf16(x):
    hi = x.astype(BF16)
    return hi, (x - hi.astype(F32)).astype(BF16)


def _router_kernel(h_ref, g_ref, r_ref, hn_ref, idx_ref, wt_ref, *, n_experts):
    hn = _rms(h_ref[...], g_ref[...], RMS_EPS)
    hn_ref[...] = hn
    x_hi, x_lo = _split_bf16(hn)
    r_hi, r_lo = _split_bf16(r_ref[...])
    logits = (jnp.dot(x_hi, r_hi, preferred_element_type=F32)
              + jnp.dot(x_lo, r_hi, preferred_element_type=F32)
              + jnp.dot(x_hi, r_lo, preferred_element_type=F32))
    lane = lax.broadcasted_iota(jnp.int32, logits.shape, 1)
    ninf = jnp.float32(-jnp.inf)
    lg = jnp.where(lane < n_experts, logits, ninf)
    v1 = jnp.max(lg, axis=-1, keepdims=True)
    i1 = jnp.min(jnp.where(lg == v1, lane, LANES), axis=-1, keepdims=True)
    lg2 = jnp.where(lane == i1, ninf, lg)
    v2 = jnp.max(lg2, axis=-1, keepdims=True)
    i2 = jnp.min(jnp.where(lg2 == v2, lane, LANES), axis=-1, keepdims=True)
    e = jnp.exp(v2 - v1)
    w1 = 1.0 / (1.0 + e)
    w2 = e / (1.0 + e)
    idx_ref[...] = jnp.where(lane == 0, i1, jnp.where(lane == 1, i2, 0))
    wt_ref[...] = jnp.where(lane == 0, w1, jnp.where(lane == 1, w2, 0.0))


def _router(h, g, router_padded, n_experts):
    rows, d = h.shape
    row_blk = lambda c: pl.BlockSpec((TB, c), lambda i: (i, 0))
    return pl.pallas_call(
        functools.partial(_router_kernel, n_experts=n_experts),
        grid=(rows // TB,),
        in_specs=[row_blk(d), pl.BlockSpec((1, d), lambda i: (0, 0)),
                  pl.BlockSpec((d, LANES), lambda i: (0, 0))],
        out_specs=[row_blk(d), row_blk(LANES), row_blk(LANES)],
        out_shape=[jax.ShapeDtypeStruct((rows, d), F32),
                   jax.ShapeDtypeStruct((rows, LANES), jnp.int32),
                   jax.ShapeDtypeStruct((rows, LANES), F32)],
        compiler_params=_cparams("parallel"),
        name="moe_router",
    )(h, g, router_padded)


def _row_copy(src_ref, dst_ref, src_row, dst_row, sem):
    return pltpu.make_async_copy(src_ref.at[pl.ds(src_row, 1), :], dst_ref.at[pl.ds(dst_row, 1), :], sem)


def _for_rows(n_rows, fn):
    def body(r, carry):
        fn(r)
        return carry

    lax.fori_loop(0, n_rows, body, 0, unroll=8)


def _idx_blocks(idx):
    return idx.reshape(idx.shape[0] // GATHER_ROWS, 1, GATHER_ROWS)


_IDX_SPEC = pl.BlockSpec((1, 1, GATHER_ROWS), lambda i: (i, 0, 0), memory_space=pltpu.SMEM)


def _dispatch_kernel(p1_ref, p2_ref, hn_ref, xs_init_hbm, xs_hbm, sem):
    del xs_init_hbm
    n = hn_ref.shape[0]

    def start(r):
        _row_copy(hn_ref, xs_hbm, r, p1_ref[0, 0, r], sem.at[0]).start()
        _row_copy(hn_ref, xs_hbm, r, p2_ref[0, 0, r], sem.at[1]).start()

    def wait(r):
        _row_copy(hn_ref, xs_hbm, r, 0, sem.at[0]).wait()
        _row_copy(hn_ref, xs_hbm, r, 0, sem.at[1]).wait()

    _for_rows(n, start)
    _for_rows(n, wait)


def _dispatch(hn, pos1, pos2, sorted_rows):
    rows, d = hn.shape
    return pl.pallas_call(
        _dispatch_kernel,
        grid=(rows // GATHER_ROWS,),
        in_specs=[_IDX_SPEC, _IDX_SPEC, pl.BlockSpec((GATHER_ROWS, d), lambda i: (i, 0)),
                  pl.BlockSpec(memory_space=pl.ANY)],
        out_specs=pl.BlockSpec(memory_space=pl.ANY),
        out_shape=jax.ShapeDtypeStruct((sorted_rows, d), hn.dtype),
        scratch_shapes=[pltpu.SemaphoreType.DMA((2,))],
        input_output_aliases={3: 0},
        compiler_params=_cparams("arbitrary"),
        name="moe_dispatch",
    )(_idx_blocks(pos1), _idx_blocks(pos2), hn, jnp.zeros((sorted_rows, d), hn.dtype))


def _moe_ffn_kernel(te_ref, tv_ref, x_ref, wg_ref, wu_ref, wd_ref, o_ref, a_ref):
    i = pl.program_id(0)

    @pl.when(tv_ref[i] > 0)
    def _():
        o_ref[...] = _swiglu(x_ref[...].astype(BF16), wg_ref, wu_ref, wd_ref, a_ref, lead=(0,))

    @pl.when(tv_ref[i] == 0)
    def _():
        o_ref[...] = jnp.zeros(o_ref.shape, F32)


def _moe_ffn(tile_expert, tile_valid, xs, wg, wu, wd):
    rows, d = xs.shape
    dff = wg.shape[2]
    expert_w = lambda a: pl.BlockSpec((1,) + a.shape[1:], lambda i, te, tv: (te[i], 0, 0),
                                      pipeline_mode=pl.Buffered(1))
    grid_spec = pltpu.PrefetchScalarGridSpec(
        num_scalar_prefetch=2,
        grid=(rows // MOE_TM,),
        in_specs=[pl.BlockSpec((MOE_TM, d), lambda i, te, tv: (i, 0)),
                  expert_w(wg), expert_w(wu), expert_w(wd)],
        out_specs=pl.BlockSpec((MOE_TM, d), lambda i, te, tv: (i, 0)),
        scratch_shapes=[pltpu.VMEM((MOE_TM, dff), BF16)],
    )
    return pl.pallas_call(
        _moe_ffn_kernel,
        grid_spec=grid_spec,
        out_shape=jax.ShapeDtypeStruct((rows, d), F32),
        compiler_params=_cparams("arbitrary"),
        name="moe_ffn",
    )(tile_expert, tile_valid, xs, wg, wu, wd)


def _combine_kernel(p1_ref, p2_ref, h_ref, wt_ref, ys_hbm, g_ref, o_ref, y1_ref, y2_ref, sem, *, final):
    n = o_ref.shape[0]

    def start(r):
        _row_copy(ys_hbm, y1_ref, p1_ref[0, 0, r], r, sem.at[0]).start()
        _row_copy(ys_hbm, y2_ref, p2_ref[0, 0, r], r, sem.at[1]).start()

    def wait(r):
        _row_copy(ys_hbm, y1_ref, 0, r, sem.at[0]).wait()
        _row_copy(ys_hbm, y2_ref, 0, r, sem.at[1]).wait()

    _for_rows(n, start)
    _for_rows(n, wait)
    wt = wt_ref[...]
    out = h_ref[...] + (wt[:, 0:1] * y1_ref[...] + wt[:, 1:2] * y2_ref[...])
    o_ref[...] = _rms(out, g_ref[...], RMS_EPS) if final else out


def _combine(h, wts, ys, pos1, pos2, final_gain):
    rows, d = h.shape
    final = final_gain is not None
    gain = final_gain if final else jnp.ones((1, d), F32)
    row_blk = lambda c: pl.BlockSpec((GATHER_ROWS, c), lambda i: (i, 0))
    return pl.pallas_call(
        functools.partial(_combine_kernel, final=final),
        grid=(rows // GATHER_ROWS,),
        in_specs=[_IDX_SPEC, _IDX_SPEC, row_blk(d), row_blk(LANES),
                  pl.BlockSpec(memory_space=pl.ANY), pl.BlockSpec((1, d), lambda i: (0, 0))],
        out_specs=row_blk(d),
        out_shape=jax.ShapeDtypeStruct((rows, d), F32),
        scratch_shapes=[pltpu.VMEM((GATHER_ROWS, d), F32), pltpu.VMEM((GATHER_ROWS, d), F32),
                        pltpu.SemaphoreType.DMA((2,))],
        compiler_params=_cparams("arbitrary"),
        name="moe_combine",
    )(_idx_blocks(pos1), _idx_blocks(pos2), h, wts, ys, gain)


def _moe_layer(h, g, router, wg, wu, wd, final_gain):
    rows, d = h.shape
    n_experts = router.shape[1]
    router_padded = jnp.pad(router, ((0, 0), (0, LANES - n_experts)))
    hn, idx, wts = _router(h, g, router_padded, n_experts)

    n_assign = rows * TOP_K
    e_flat = idx[:, :TOP_K].reshape(n_assign)
    onehot = (e_flat[:, None] == jnp.arange(n_experts, dtype=jnp.int32)[None, :]).astype(jnp.int32)
    csum = jnp.cumsum(onehot, axis=0)
    rank = jnp.take_along_axis(csum, e_flat[:, None], axis=1)[:, 0] - 1
    counts = csum[-1]
    padded = ((counts + MOE_TM - 1) // MOE_TM) * MOE_TM
    ends = jnp.cumsum(padded)
    pos = (ends - padded)[e_flat] + rank
    n_tiles = n_assign // MOE_TM + n_experts
    tile_start = jnp.arange(n_tiles, dtype=jnp.int32) * MOE_TM
    tile_valid = (tile_start < ends[-1]).astype(jnp.int32)
    tile_expert = jnp.searchsorted(ends, jnp.minimum(tile_start, ends[-1] - 1), side="right").astype(jnp.int32)

    pos2d = pos.reshape(rows, TOP_K)
    xs = _dispatch(hn, pos2d[:, 0], pos2d[:, 1], n_tiles * MOE_TM)
    ys = _moe_ffn(tile_expert, tile_valid, xs, wg, wu, wd)
    return _combine(h, wts, ys, pos2d[:, 0], pos2d[:, 1], final_gain)


def _final_norm_kernel(h_ref, g_ref, o_ref):
    o_ref[...] = _rms(h_ref[...], g_ref[...], RMS_EPS)


def _final_norm(h, g):
    rows, d = h.shape
    return pl.pallas_call(
        _final_norm_kernel,
        grid=(rows // TB,),
        in_specs=[pl.BlockSpec((TB, d), lambda i: (i, 0)), pl.BlockSpec((1, d), lambda i: (0, 0))],
        out_specs=pl.BlockSpec((TB, d), lambda i: (i, 0)),
        out_shape=jax.ShapeDtypeStruct((rows, d), F32),
        compiler_params=_cparams("parallel"),
        name="final_norm",
    )(h, g)


def kernel(x, meta_tokens, rel_bias, norm_mix, w_in, pool_group_w, pool_scale, lambda_q1, lambda_k1, lambda_q2, lambda_k2, subln_gain, w_pool_up, w_attn_up, w_out, norm_ffn, dense_w_gate, dense_w_up, dense_w_down, moe_router, moe_w_gate, moe_w_up, moe_w_down, final_norm):
    batch, seq, d = x.shape
    depth = w_in.shape[0]
    seq_all = N_META + seq
    lp = -(-seq_all // TB) * TB
    nt = lp // TB
    pw = len(POOL_WINDOWS) * POOL_GROUP_DIM
    qw = DIFF_HEADS * 2 * DIFF_QK_DIM
    assert (batch * lp) % GATHER_ROWS == 0 and TB % HALO == 0 and TB % POOL_SUB == 0

    meta = jnp.broadcast_to(meta_tokens[None].astype(x.dtype), (batch, N_META, d))
    h = jnp.concatenate([meta, x, jnp.zeros((batch, lp - seq_all, d), x.dtype)], axis=1)
    h = h.reshape(batch * lp, d)

    dbias, sbias = _bias_tiles(rel_bias.astype(F32))
    bands = jnp.asarray(_pool_bands(), BF16)
    row = lambda a: a.reshape(1, -1).astype(F32)

    for layer in range(depth):
        u, q, k, v, gates = _inproj(h, row(norm_mix[layer]), w_in[layer].astype(BF16), pw, qw)
        lambda_init = 0.8 - 0.6 * math.exp(-0.3 * layer)
        b_out = _attention(q, k, v, dbias, sbias, row(lambda_q1[layer]), row(lambda_k1[layer]),
                           row(lambda_q2[layer]), row(lambda_k2[layer]), row(subln_gain[layer]),
                           batch=batch, lp=lp, lambda_init=lambda_init)
        h = _merge(h, u, b_out, gates, bands, pool_group_w[layer].astype(BF16), row(pool_scale[layer]),
                   w_pool_up[layer].astype(BF16), w_attn_up[layer].astype(BF16), w_out[layer].astype(BF16),
                   nt=nt)
        j = layer // 2
        if layer % 2 == 0:
            h = _dense_ffn(h, row(norm_ffn[layer]), dense_w_gate[j].astype(BF16),
                           dense_w_up[j].astype(BF16), dense_w_down[j].astype(BF16))
        else:
            h = _moe_layer(h, row(norm_ffn[layer]), moe_router[j].astype(F32), moe_w_gate[j].astype(BF16),
                           moe_w_up[j].astype(BF16), moe_w_down[j].astype(BF16),
                           row(final_norm) if layer == depth - 1 else None)

    out = h if depth % 2 == 0 else _final_norm(h, row(final_norm))
    return out.reshape(batch, lp, d)[:, N_META:seq_all]
```

```python
import functools
import math

import numpy as np
import jax
import jax.numpy as jnp
from jax import lax
from jax.experimental import pallas as pl
from jax.experimental.pallas import tpu as pltpu

F32 = jnp.float32
BF16 = jnp.bfloat16

N_META = 16
POOL_WINDOWS = (2, 4, 8, 16)
POOL_GROUP_DIM = 128
DIFF_HEADS = 4
DIFF_QK_DIM = 64
DIFF_V_DIM = 128
REL_BUCKETS = 32
REL_MAX_DIST = 128
TOP_K = 2
RMS_EPS = 1e-6
SUBLN_EPS = 1e-5

LANES = 128
TB = 768
ATT_UNIT = 128
HALO = 128
POOL_SUB = 256
FF_CHUNK = 512
MOE_TM = 512
GATHER_ROWS = 512
VMEM_LIMIT = 56 * 1024 * 1024
LOG2E = 1.4426950408889634
NEG = -1e30


def _cparams(*sem):
    return pltpu.CompilerParams(dimension_semantics=sem, vmem_limit_bytes=VMEM_LIMIT)


def _rms(x, g, eps):
    ms = jnp.mean(x * x, axis=-1, keepdims=True)
    return x * lax.rsqrt(ms + eps) * g


def _inproj_kernel(h_ref, g_ref, w_ref, u_ref, q_ref, k_ref, v_ref, gate_ref, *, pw, qw):
    hn = _rms(h_ref[...], g_ref[...], RMS_EPS).astype(BF16)

    def proj(lo, hi):
        return jnp.dot(hn, w_ref[:, lo:hi], preferred_element_type=F32)

    u_ref[...] = proj(0, pw).astype(BF16)
    q_ref[...] = (proj(pw, pw + qw) * (DIFF_QK_DIM ** -0.5 * LOG2E)).astype(BF16)
    k_ref[...] = proj(pw + qw, pw + 2 * qw).astype(BF16)
    v = proj(pw + 2 * qw, pw + 3 * qw).astype(BF16)
    ones = jnp.ones((v.shape[0], DIFF_V_DIM), BF16)
    v_ref[...] = jnp.concatenate(
        [blk for h in range(DIFF_HEADS) for blk in (v[:, h * DIFF_V_DIM:(h + 1) * DIFF_V_DIM], ones)], axis=1)
    gate_ref[...] = jax.nn.sigmoid(proj(pw + 3 * qw, w_ref.shape[1])).astype(BF16)


def _inproj(h, g, w, pw, qw):
    rows, d = h.shape
    cols = w.shape[1]
    gw = cols - pw - 3 * qw
    row_blk = lambda c: pl.BlockSpec((TB, c), lambda i: (i, 0))
    return pl.pallas_call(
        functools.partial(_inproj_kernel, pw=pw, qw=qw),
        grid=(rows // TB,),
        in_specs=[row_blk(d),
                  pl.BlockSpec((1, d), lambda i: (0, 0)),
                  pl.BlockSpec((d, cols), lambda i: (0, 0), pipeline_mode=pl.Buffered(1))],
        out_specs=[row_blk(pw), row_blk(qw), row_blk(qw), row_blk(2 * qw), row_blk(gw)],
        out_shape=[jax.ShapeDtypeStruct((rows, c), BF16) for c in (pw, qw, qw, 2 * qw, gw)],
        compiler_params=_cparams("parallel"),
        name="inproj",
    )(h, g, w)


def _bucket_starts():
    max_exact = REL_BUCKETS // 2
    n = np.arange(1, REL_MAX_DIST + 1)
    large = max_exact + (np.log(n.astype(np.float32) / max_exact) / np.float32(math.log(REL_MAX_DIST / max_exact))
                         * (REL_BUCKETS - max_exact)).astype(np.int32)
    bucket = np.where(n < max_exact, n, np.minimum(large, REL_BUCKETS - 1))
    return [0] + [int(n[bucket >= b].min()) for b in range(1, REL_BUCKETS)]


def _bias_tiles_kernel(tab_ref, diag_ref, sub_ref, *, starts):
    h = pl.program_id(0)
    far = tab_ref[REL_BUCKETS - 1, h]
    row = lax.broadcasted_iota(jnp.int32, (ATT_UNIT, ATT_UNIT), 0)
    col = lax.broadcasted_iota(jnp.int32, (ATT_UNIT, ATT_UNIT), 1)
    dist = row - col

    def bias_of(n):
        t = jnp.full(n.shape, (tab_ref[0, h] - far) * LOG2E, F32)
        for b in range(1, REL_BUCKETS):
            t = jnp.where(n >= starts[b], (tab_ref[b, h] - far) * LOG2E, t)
        return t

    diag_ref[0] = jnp.where(dist >= 0, bias_of(dist), NEG)
    sub_ref[0] = bias_of(dist + ATT_UNIT)


def _bias_tiles(rel_bias):
    starts = _bucket_starts()
    assert ATT_UNIT >= starts[-1]
    heads = rel_bias.shape[1]
    tile = pl.BlockSpec((1, ATT_UNIT, ATT_UNIT), lambda h: (h, 0, 0))
    return pl.pallas_call(
        functools.partial(_bias_tiles_kernel, starts=starts),
        grid=(heads,),
        in_specs=[pl.BlockSpec(memory_space=pltpu.SMEM)],
        out_specs=[tile, tile],
        out_shape=[jax.ShapeDtypeStruct((heads, ATT_UNIT, ATT_UNIT), F32)] * 2,
        compiler_params=_cparams("parallel"),
        name="bias_tiles",
    )(rel_bias)


def _attn_kernel(q_ref, k_ref, v_ref, dbias_ref, sbias_ref, lq1_ref, lk1_ref, lq2_ref, lk2_ref,
                 gain_ref, o_ref, m_ref, acc_ref, *, lambda_init):
    i = pl.program_id(1)
    heads = q_ref.shape[1] // (2 * DIFF_QK_DIM)
    q = q_ref[...]
    lane = lax.broadcasted_iota(jnp.int32, (TB, 2 * DIFF_QK_DIM), 1)
    qs = []
    for h in range(heads):
        qh = q[:, h * 2 * DIFF_QK_DIM:(h + 1) * 2 * DIFF_QK_DIM]
        zero = jnp.zeros_like(qh)
        qs.append((jnp.where(lane < DIFF_QK_DIM, qh, zero), jnp.where(lane >= DIFF_QK_DIM, qh, zero)))

    m_ref[...] = jnp.full(m_ref.shape, NEG, F32)
    acc_ref[...] = jnp.zeros(acc_ref.shape, F32)

    def step(j, kind):
        start = pl.multiple_of(j * TB, TB)
        for h in range(heads):
            kj = k_ref[pl.ds(start, TB), h * 2 * DIFF_QK_DIM:(h + 1) * 2 * DIFF_QK_DIM]
            vj = v_ref[pl.ds(start, TB), h * 2 * DIFF_V_DIM:(h + 1) * 2 * DIFF_V_DIM]
            for u in range(TB // ATT_UNIT):
                rows = slice(u * ATT_UNIT, (u + 1) * ATT_UNIT)
                n = (u + 1) * ATT_UNIT if kind == "diag" else TB
                for c in range(2):
                    slot = 2 * h + c
                    s = lax.dot_general(qs[h][c][rows], kj[:n], (((1,), (1,)), ((), ())),
                                        preferred_element_type=F32)
                    cols = [s[:, b * ATT_UNIT:(b + 1) * ATT_UNIT] for b in range(n // ATT_UNIT)]
                    if kind == "diag":
                        cols[u] = cols[u] + dbias_ref[h]
                        if u >= 1:
                            cols[u - 1] = cols[u - 1] + sbias_ref[h]
                    elif kind == "prev" and u == 0:
                        cols[-1] = cols[-1] + sbias_ref[h]
                    m_prev = m_ref[slot, rows, :]
                    m_cur = jnp.max(functools.reduce(jnp.maximum, cols), axis=-1, keepdims=True)
                    m_new = jnp.maximum(m_prev, m_cur)
                    alpha = jnp.exp2(m_prev - m_new)
                    ps = [jnp.exp2(blk - m_new).astype(BF16) for blk in cols]
                    p = ps[0] if len(ps) == 1 else jnp.concatenate(ps, axis=1)
                    pv = jnp.dot(p, vj[:n], preferred_element_type=F32)
                    acc_ref[slot, rows, :] = jnp.concatenate([alpha, alpha], axis=1) * acc_ref[slot, rows, :] + pv
                    m_ref[slot, rows, :] = m_new

    def far(j, carry):
        step(j, "far")
        return carry

    lax.fori_loop(0, jnp.maximum(i - 1, 0), far, 0)

    @pl.when(i >= 1)
    def _():
        step(i - 1, "prev")

    step(i, "diag")

    lam = (jnp.exp(jnp.sum(lq1_ref[...] * lk1_ref[...], axis=-1, keepdims=True))
           - jnp.exp(jnp.sum(lq2_ref[...] * lk2_ref[...], axis=-1, keepdims=True))
           + lambda_init)
    for h in range(heads):
        a1, a2 = acc_ref[2 * h], acc_ref[2 * h + 1]
        o = a1[:, :DIFF_V_DIM] / a1[:, DIFF_V_DIM:] - lam * (a2[:, :DIFF_V_DIM] / a2[:, DIFF_V_DIM:])
        o_ref[:, h * DIFF_V_DIM:(h + 1) * DIFF_V_DIM] = (
            _rms(o, gain_ref[...], SUBLN_EPS) * (1.0 - lambda_init)).astype(BF16)


def _attention(q, k, v, dbias, sbias, lq1, lk1, lq2, lk2, gain, *, batch, lp, lambda_init):
    rows, width = q.shape
    heads = width // (2 * DIFF_QK_DIM)
    nq = lp // TB
    q_spec = pl.BlockSpec((TB, width), lambda b, i: (b * nq + i, 0))
    o_spec = pl.BlockSpec((TB, heads * DIFF_V_DIM), lambda b, i: (b * nq + i, 0))
    k_spec = pl.BlockSpec((lp, width), lambda b, i: (b, 0), pipeline_mode=pl.Buffered(1))
    v_spec = pl.BlockSpec((lp, v.shape[1]), lambda b, i: (b, 0), pipeline_mode=pl.Buffered(1))
    bias_spec = pl.BlockSpec((heads, ATT_UNIT, ATT_UNIT), lambda b, i: (0, 0, 0))
    vec_spec = lambda n: pl.BlockSpec((1, n), lambda b, i: (0, 0))
    return pl.pallas_call(
        functools.partial(_attn_kernel, lambda_init=lambda_init),
        grid=(batch, nq),
        in_specs=[q_spec, k_spec, v_spec, bias_spec, bias_spec,
                  vec_spec(DIFF_QK_DIM), vec_spec(DIFF_QK_DIM), vec_spec(DIFF_QK_DIM),
                  vec_spec(DIFF_QK_DIM), vec_spec(DIFF_V_DIM)],
        out_specs=o_spec,
        out_shape=jax.ShapeDtypeStruct((rows, heads * DIFF_V_DIM), BF16),
        scratch_shapes=[pltpu.VMEM((2 * heads, TB, LANES), F32),
                        pltpu.VMEM((2 * heads, TB, 2 * DIFF_V_DIM), F32)],
        compiler_params=_cparams("parallel", "arbitrary"),
        name="diff_attn",
    )(q, k, v, dbias, sbias, lq1, lk1, lq2, lk2, gain)


def _pool_bands():
    r = np.arange(POOL_SUB)[:, None] + HALO
    c = np.arange(POOL_SUB + HALO)[None, :]
    d = r - c
    return np.stack([((d >= 0) & (d < w)).astype(np.float32) for w in POOL_WINDOWS])


def _merge_kernel(h_ref, u_ref, halo_ref, b_ref, gate_ref, band_ref, gw_ref, scale_ref,
                  wpu_ref, wau_ref, wo_ref, o_ref, *, nt):
    t = pl.program_id(0) % nt
    u = u_ref[...]
    halo = halo_ref[...]
    halo = jnp.where(t > 0, halo, jnp.zeros_like(halo))
    xcat = jnp.concatenate([halo, u], axis=0)
    uf = u.astype(F32)
    pos = t * TB + lax.broadcasted_iota(jnp.int32, (TB, 1), 0)
    d_model = o_ref.shape[1]

    ys = []
    for g, w in enumerate(POOL_WINDOWS):
        lo, hi = g * POOL_GROUP_DIM, (g + 1) * POOL_GROUP_DIM
        cnt = jnp.minimum(pos + 1, w).astype(F32)
        sums = [jnp.dot(band_ref[g], xcat[s * POOL_SUB:s * POOL_SUB + POOL_SUB + HALO, lo:hi],
                        preferred_element_type=F32) for s in range(TB // POOL_SUB)]
        mixed = (jnp.concatenate(sums, axis=0) / cnt - uf[:, lo:hi]).astype(BF16)
        ys.append(jnp.dot(mixed, gw_ref[g], preferred_element_type=F32))
    a_out = (jnp.concatenate(ys, axis=1) * scale_ref[...]).astype(BF16)
    pa = jnp.dot(a_out, wpu_ref[...], preferred_element_type=F32)
    pb = jnp.dot(b_ref[...], wau_ref[...], preferred_element_type=F32)
    gates = gate_ref[...]
    merged = (gates[:, :d_model].astype(F32) * pa + gates[:, d_model:].astype(F32) * pb).astype(BF16)
    o_ref[...] = h_ref[...] + jnp.dot(merged, wo_ref[...], preferred_element_type=F32)


def _merge(h, u, b_out, gates, bands, gw, scale, wpu, wau, wo, *, nt):
    rows, d = h.shape
    pw = u.shape[1]
    halo_per_tile = TB // HALO
    row_blk = lambda c: pl.BlockSpec((TB, c), lambda i: (i, 0))
    full = lambda a: pl.BlockSpec(a.shape, lambda i: (0,) * a.ndim)
    return pl.pallas_call(
        functools.partial(_merge_kernel, nt=nt),
        grid=(rows // TB,),
        in_specs=[row_blk(d), row_blk(pw),
                  pl.BlockSpec((HALO, pw), lambda i: (jnp.maximum(i * halo_per_tile - 1, 0), 0)),
                  row_blk(b_out.shape[1]), row_blk(gates.shape[1]),
                  full(bands), full(gw), full(scale), full(wpu), full(wau), full(wo)],
        out_specs=row_blk(d),
        out_shape=jax.ShapeDtypeStruct((rows, d), F32),
        compiler_params=_cparams("parallel"),
        name="pool_merge",
    )(h, u, u, b_out, gates, bands, gw, scale, wpu, wau, wo)


def _swiglu(x, wg_ref, wu_ref, wd_ref, a_ref, lead=()):
    dff = a_ref.shape[1]
    for lo in range(0, dff, FF_CHUNK):
        cols = lead + (slice(None), slice(lo, lo + FF_CHUNK))
        g = jnp.dot(x, wg_ref[cols], preferred_element_type=F32)
        u = jnp.dot(x, wu_ref[cols], preferred_element_type=F32)
        a_ref[:, lo:lo + FF_CHUNK] = (g * jax.nn.sigmoid(g) * u).astype(BF16)
    return jnp.dot(a_ref[...], wd_ref[lead + (slice(None), slice(None))], preferred_element_type=F32)


def _dense_ffn_kernel(h_ref, g_ref, wg_ref, wu_ref, wd_ref, o_ref, a_ref):
    h = h_ref[...]
    hn = _rms(h, g_ref[...], RMS_EPS).astype(BF16)
    o_ref[...] = h + _swiglu(hn, wg_ref, wu_ref, wd_ref, a_ref)


def _dense_ffn(h, g, wg, wu, wd):
    rows, d = h.shape
    dff = wg.shape[1]
    resident = lambda a: pl.BlockSpec(a.shape, lambda i: (0, 0), pipeline_mode=pl.Buffered(1))
    return pl.pallas_call(
        _dense_ffn_kernel,
        grid=(rows // TB,),
        in_specs=[pl.BlockSpec((TB, d), lambda i: (i, 0)),
                  pl.BlockSpec((1, d), lambda i: (0, 0)),
                  resident(wg), resident(wu), resident(wd)],
        out_specs=pl.BlockSpec((TB, d), lambda i: (i, 0)),
        out_shape=jax.ShapeDtypeStruct((rows, d), F32),
        scratch_shapes=[pltpu.VMEM((TB, dff), BF16)],
        compiler_params=_cparams("parallel"),
        name="dense_ffn",
    )(h, g, wg, wu, wd)


def _split_bf16(x):
    hi = x.astype(BF16)
    return hi, (x - hi.astype(F32)).astype(BF16)


def _router_kernel(h_ref, g_ref, r_ref, hn_ref, idx_ref, wt_ref, *, n_experts):
    hn = _rms(h_ref[...], g_ref[...], RMS_EPS)
    hn_ref[...] = hn
    x_hi, x_lo = _split_bf16(hn)
    r_hi, r_lo = _split_bf16(r_ref[...])
    logits = (jnp.dot(x_hi, r_hi, preferred_element_type=F32)
              + jnp.dot(x_lo, r_hi, preferred_element_type=F32)
              + jnp.dot(x_hi, r_lo, preferred_element_type=F32))
    lane = lax.broadcasted_iota(jnp.int32, logits.shape, 1)
    ninf = jnp.float32(-jnp.inf)
    lg = jnp.where(lane < n_experts, logits, ninf)
    v1 = jnp.max(lg, axis=-1, keepdims=True)
    i1 = jnp.min(jnp.where(lg == v1, lane, LANES), axis=-1, keepdims=True)
    lg2 = jnp.where(lane == i1, ninf, lg)
    v2 = jnp.max(lg2, axis=-1, keepdims=True)
    i2 = jnp.min(jnp.where(lg2 == v2, lane, LANES), axis=-1, keepdims=True)
    e = jnp.exp(v2 - v1)
    w1 = 1.0 / (1.0 + e)
    w2 = e / (1.0 + e)
    idx_ref[...] = jnp.where(lane == 0, i1, jnp.where(lane == 1, i2, 0))
    wt_ref[...] = jnp.where(lane == 0, w1, jnp.where(lane == 1, w2, 0.0))


def _router(h, g, router_padded, n_experts):
    rows, d = h.shape
    row_blk = lambda c: pl.BlockSpec((TB, c), lambda i: (i, 0))
    return pl.pallas_call(
        functools.partial(_router_kernel, n_experts=n_experts),
        grid=(rows // TB,),
        in_specs=[row_blk(d), pl.BlockSpec((1, d), lambda i: (0, 0)),
                  pl.BlockSpec((d, LANES), lambda i: (0, 0))],
        out_specs=[row_blk(d), row_blk(LANES), row_blk(LANES)],
        out_shape=[jax.ShapeDtypeStruct((rows, d), F32),
                   jax.ShapeDtypeStruct((rows, LANES), jnp.int32),
                   jax.ShapeDtypeStruct((rows, LANES), F32)],
        compiler_params=_cparams("parallel"),
        name="moe_router",
    )(h, g, router_padded)


def _row_copy(src_ref, dst_ref, src_row, dst_row, sem):
    return pltpu.make_async_copy(src_ref.at[pl.ds(src_row, 1), :], dst_ref.at[pl.ds(dst_row, 1), :], sem)


def _for_rows(n_rows, fn):
    def body(r, carry):
        fn(r)
        return carry

    lax.fori_loop(0, n_rows, body, 0, unroll=8)


def _idx_blocks(idx):
    return idx.reshape(idx.shape[0] // GATHER_ROWS, 1, GATHER_ROWS)


_IDX_SPEC = pl.BlockSpec((1, 1, GATHER_ROWS), lambda i: (i, 0, 0), memory_space=pltpu.SMEM)


def _dispatch_kernel(p1_ref, p2_ref, hn_ref, xs_init_hbm, xs_hbm, sem):
    del xs_init_hbm
    n = hn_ref.shape[0]

    def start(r):
        _row_copy(hn_ref, xs_hbm, r, p1_ref[0, 0, r], sem.at[0]).start()
        _row_copy(hn_ref, xs_hbm, r, p2_ref[0, 0, r], sem.at[1]).start()

    def wait(r):
        _row_copy(hn_ref, xs_hbm, r, 0, sem.at[0]).wait()
        _row_copy(hn_ref, xs_hbm, r, 0, sem.at[1]).wait()

    _for_rows(n, start)
    _for_rows(n, wait)


def _dispatch(hn, pos1, pos2, sorted_rows):
    rows, d = hn.shape
    return pl.pallas_call(
        _dispatch_kernel,
        grid=(rows // GATHER_ROWS,),
        in_specs=[_IDX_SPEC, _IDX_SPEC, pl.BlockSpec((GATHER_ROWS, d), lambda i: (i, 0)),
                  pl.BlockSpec(memory_space=pl.ANY)],
        out_specs=pl.BlockSpec(memory_space=pl.ANY),
        out_shape=jax.ShapeDtypeStruct((sorted_rows, d), hn.dtype),
        scratch_shapes=[pltpu.SemaphoreType.DMA((2,))],
        input_output_aliases={3: 0},
        compiler_params=_cparams("arbitrary"),
        name="moe_dispatch",
    )(_idx_blocks(pos1), _idx_blocks(pos2), hn, jnp.zeros((sorted_rows, d), hn.dtype))


def _moe_ffn_kernel(te_ref, tv_ref, x_ref, wg_ref, wu_ref, wd_ref, o_ref, a_ref):
    i = pl.program_id(0)

    @pl.when(tv_ref[i] > 0)
    def _():
        o_ref[...] = _swiglu(x_ref[...].astype(BF16), wg_ref, wu_ref, wd_ref, a_ref, lead=(0,))

    @pl.when(tv_ref[i] == 0)
    def _():
        o_ref[...] = jnp.zeros(o_ref.shape, F32)


def _moe_ffn(tile_expert, tile_valid, xs, wg, wu, wd):
    rows, d = xs.shape
    dff = wg.shape[2]
    expert_w = lambda a: pl.BlockSpec((1,) + a.shape[1:], lambda i, te, tv: (te[i], 0, 0),
                                      pipeline_mode=pl.Buffered(1))
    grid_spec = pltpu.PrefetchScalarGridSpec(
        num_scalar_prefetch=2,
        grid=(rows // MOE_TM,),
        in_specs=[pl.BlockSpec((MOE_TM, d), lambda i, te, tv: (i, 0)),
                  expert_w(wg), expert_w(wu), expert_w(wd)],
        out_specs=pl.BlockSpec((MOE_TM, d), lambda i, te, tv: (i, 0)),
        scratch_shapes=[pltpu.VMEM((MOE_TM, dff), BF16)],
    )
    return pl.pallas_call(
        _moe_ffn_kernel,
        grid_spec=grid_spec,
        out_shape=jax.ShapeDtypeStruct((rows, d), F32),
        compiler_params=_cparams("arbitrary"),
        name="moe_ffn",
    )(tile_expert, tile_valid, xs, wg, wu, wd)


def _combine_kernel(p1_ref, p2_ref, h_ref, wt_ref, ys_hbm, g_ref, o_ref, y1_ref, y2_ref, sem, *, final):
    n = o_ref.shape[0]

    def start(r):
        _row_copy(ys_hbm, y1_ref, p1_ref[0, 0, r], r, sem.at[0]).start()
        _row_copy(ys_hbm, y2_ref, p2_ref[0, 0, r], r, sem.at[1]).start()

    def wait(r):
        _row_copy(ys_hbm, y1_ref, 0, r, sem.at[0]).wait()
        _row_copy(ys_hbm, y2_ref, 0, r, sem.at[1]).wait()

    _for_rows(n, start)
    _for_rows(n, wait)
    wt = wt_ref[...]
    out = h_ref[...] + (wt[:, 0:1] * y1_ref[...] + wt[:, 1:2] * y2_ref[...])
    o_ref[...] = _rms(out, g_ref[...], RMS_EPS) if final else out


def _combine(h, wts, ys, pos1, pos2, final_gain):
    rows, d = h.shape
    final = final_gain is not None
    gain = final_gain if final else jnp.ones((1, d), F32)
    row_blk = lambda c: pl.BlockSpec((GATHER_ROWS, c), lambda i: (i, 0))
    return pl.pallas_call(
        functools.partial(_combine_kernel, final=final),
        grid=(rows // GATHER_ROWS,),
        in_specs=[_IDX_SPEC, _IDX_SPEC, row_blk(d), row_blk(LANES),
                  pl.BlockSpec(memory_space=pl.ANY), pl.BlockSpec((1, d), lambda i: (0, 0))],
        out_specs=row_blk(d),
        out_shape=jax.ShapeDtypeStruct((rows, d), F32),
        scratch_shapes=[pltpu.VMEM((GATHER_ROWS, d), F32), pltpu.VMEM((GATHER_ROWS, d), F32),
                        pltpu.SemaphoreType.DMA((2,))],
        compiler_params=_cparams("arbitrary"),
        name="moe_combine",
    )(_idx_blocks(pos1), _idx_blocks(pos2), h, wts, ys, gain)


def _moe_layer(h, g, router, wg, wu, wd, final_gain):
    rows, d = h.shape
    n_experts = router.shape[1]
    router_padded = jnp.pad(router, ((0, 0), (0, LANES - n_experts)))
    hn, idx, wts = _router(h, g, router_padded, n_experts)

    n_assign = rows * TOP_K
    e_flat = idx[:, :TOP_K].reshape(n_assign)
    onehot = (e_flat[:, None] == jnp.arange(n_experts, dtype=jnp.int32)[None, :]).astype(jnp.int32)
    csum = jnp.cumsum(onehot, axis=0)
    rank = jnp.take_along_axis(csum, e_flat[:, None], axis=1)[:, 0] - 1
    counts = csum[-1]
    padded = ((counts + MOE_TM - 1) // MOE_TM) * MOE_TM
    ends = jnp.cumsum(padded)
    pos = (ends - padded)[e_flat] + rank
    n_tiles = n_assign // MOE_TM + n_experts
    tile_start = jnp.arange(n_tiles, dtype=jnp.int32) * MOE_TM
    tile_valid = (tile_start < ends[-1]).astype(jnp.int32)
    tile_expert = jnp.searchsorted(ends, jnp.minimum(tile_start, ends[-1] - 1), side="right").astype(jnp.int32)

    pos2d = pos.reshape(rows, TOP_K)
    xs = _dispatch(hn, pos2d[:, 0], pos2d[:, 1], n_tiles * MOE_TM)
    ys = _moe_ffn(tile_expert, tile_valid, xs, wg, wu, wd)
    return _combine(h, wts, ys, pos2d[:, 0], pos2d[:, 1], final_gain)


def _final_norm_kernel(h_ref, g_ref, o_ref):
    o_ref[...] = _rms(h_ref[...], g_ref[...], RMS_EPS)


def _final_norm(h, g):
    rows, d = h.shape
    return pl.pallas_call(
        _final_norm_kernel,
        grid=(rows // TB,),
        in_specs=[pl.BlockSpec((TB, d), lambda i: (i, 0)), pl.BlockSpec((1, d), lambda i: (0, 0))],
        out_specs=pl.BlockSpec((TB, d), lambda i: (i, 0)),
        out_shape=jax.ShapeDtypeStruct((rows, d), F32),
        compiler_params=_cparams("parallel"),
        name="final_norm",
    )(h, g)


def kernel(x, meta_tokens, rel_bias, norm_mix, w_in, pool_group_w, pool_scale, lambda_q1, lambda_k1, lambda_q2, lambda_k2, subln_gain, w_pool_up, w_attn_up, w_out, norm_ffn, dense_w_gate, dense_w_up, dense_w_down, moe_router, moe_w_gate, moe_w_up, moe_w_down, final_norm):
    batch, seq, d = x.shape
    depth = w_in.shape[0]
    seq_all = N_META + seq
    lp = -(-seq_all // TB) * TB
    nt = lp // TB
    pw = len(POOL_WINDOWS) * POOL_GROUP_DIM
    qw = DIFF_HEADS * 2 * DIFF_QK_DIM
    assert (batch * lp) % GATHER_ROWS == 0 and TB % HALO == 0 and TB % POOL_SUB == 0

    meta = jnp.broadcast_to(meta_tokens[None].astype(x.dtype), (batch, N_META, d))
    h = jnp.concatenate([meta, x, jnp.zeros((batch, lp - seq_all, d), x.dtype)], axis=1)
    h = h.reshape(batch * lp, d)

    dbias, sbias = _bias_tiles(rel_bias.astype(F32))
    bands = jnp.asarray(_pool_bands(), BF16)
    row = lambda a: a.reshape(1, -1).astype(F32)

    for layer in range(depth):
        u, q, k, v, gates = _inproj(h, row(norm_mix[layer]), w_in[layer].astype(BF16), pw, qw)
        lambda_init = 0.8 - 0.6 * math.exp(-0.3 * layer)
        b_out = _attention(q, k, v, dbias, sbias, row(lambda_q1[layer]), row(lambda_k1[layer]),
                           row(lambda_q2[layer]), row(lambda_k2[layer]), row(subln_gain[layer]),
                           batch=batch, lp=lp, lambda_init=lambda_init)
        h = _merge(h, u, b_out, gates, bands, pool_group_w[layer].astype(BF16), row(pool_scale[layer]),
                   w_pool_up[layer].astype(BF16), w_attn_up[layer].astype(BF16), w_out[layer].astype(BF16),
                   nt=nt)
        j = layer // 2
        if layer % 2 == 0:
            h = _dense_ffn(h, row(norm_ffn[layer]), dense_w_gate[j].astype(BF16),
                           dense_w_up[j].astype(BF16), dense_w_down[j].astype(BF16))
        else:
            h = _moe_layer(h, row(norm_ffn[layer]), moe_router[j].astype(F32), moe_w_gate[j].astype(BF16),
                           moe_w_up[j].astype(BF16), moe_w_down[j].astype(BF16),
                           row(final_norm) if layer == depth - 1 else None)

    out = h if depth % 2 == 0 else _final_norm(h, row(final_norm))
    return out.reshape(batch, lp, d)[:, N_META:seq_all]
```

```python
import functools
import math

import numpy as np
import jax
import jax.numpy as jnp
from jax import lax
from jax.experimental import pallas as pl
from jax.experimental.pallas import tpu as pltpu

F32 = jnp.float32
BF16 = jnp.bfloat16

N_META = 16
POOL_WINDOWS = (2, 4, 8, 16)
POOL_GROUP_DIM = 128
DIFF_HEADS = 4
DIFF_QK_DIM = 64
DIFF_V_DIM = 128
REL_BUCKETS = 32
REL_MAX_DIST = 128
TOP_K = 2
RMS_EPS = 1e-6
SUBLN_EPS = 1e-5

LANES = 128
SUBLANES = 8
TB = 768
ATT_UNIT = 128
HALO = 128
POOL_SUB = 256
FF_CHUNK = 512
MOE_TM = 512
GATHER_ROWS = 512
VMEM_LIMIT = 56 * 1024 * 1024
LOG2E = 1.4426950408889634
NEG = -1e30


def _cparams(*sem):
    return pltpu.CompilerParams(dimension_semantics=sem, vmem_limit_bytes=VMEM_LIMIT)


def _rms(x, g, eps):
    ms = jnp.mean(x * x, axis=-1, keepdims=True)
    return x * lax.rsqrt(ms + eps) * g


def _inproj_kernel(h_ref, g_ref, w_ref, u_ref, q_ref, k_ref, v_ref, gate_ref, *, pw, qw):
    hn = _rms(h_ref[...], g_ref[...], RMS_EPS).astype(BF16)

    def proj(lo, hi):
        return jnp.dot(hn, w_ref[:, lo:hi], preferred_element_type=F32)

    u_ref[...] = proj(0, pw).astype(BF16)
    q_ref[...] = (proj(pw, pw + qw) * (DIFF_QK_DIM ** -0.5 * LOG2E)).astype(BF16)
    k_ref[...] = proj(pw + qw, pw + 2 * qw).astype(BF16)
    v = proj(pw + 2 * qw, pw + 3 * qw).astype(BF16)
    ones = jnp.ones((v.shape[0], DIFF_V_DIM), BF16)
    v_ref[...] = jnp.concatenate(
        [blk for h in range(DIFF_HEADS) for blk in (v[:, h * DIFF_V_DIM:(h + 1) * DIFF_V_DIM], ones)], axis=1)
    gate_ref[...] = jax.nn.sigmoid(proj(pw + 3 * qw, w_ref.shape[1])).astype(BF16)


def _inproj(h, g, w, pw, qw):
    rows, d = h.shape
    cols = w.shape[1]
    gw = cols - pw - 3 * qw
    row_blk = lambda c: pl.BlockSpec((TB, c), lambda i: (i, 0))
    return pl.pallas_call(
        functools.partial(_inproj_kernel, pw=pw, qw=qw),
        grid=(rows // TB,),
        in_specs=[row_blk(d),
                  pl.BlockSpec((1, d), lambda i: (0, 0)),
                  pl.BlockSpec((d, cols), lambda i: (0, 0), pipeline_mode=pl.Buffered(1))],
        out_specs=[row_blk(pw), row_blk(qw), row_blk(qw), row_blk(2 * qw), row_blk(gw)],
        out_shape=[jax.ShapeDtypeStruct((rows, c), BF16) for c in (pw, qw, qw, 2 * qw, gw)],
        compiler_params=_cparams("parallel"),
        name="inproj",
    )(h, g, w)


def _bucket_starts():
    max_exact = REL_BUCKETS // 2
    n = np.arange(1, REL_MAX_DIST + 1)
    large = max_exact + (np.log(n.astype(np.float32) / max_exact) / np.float32(math.log(REL_MAX_DIST / max_exact))
                         * (REL_BUCKETS - max_exact)).astype(np.int32)
    bucket = np.where(n < max_exact, n, np.minimum(large, REL_BUCKETS - 1))
    return [0] + [int(n[bucket >= b].min()) for b in range(1, REL_BUCKETS)]


def _bias_tiles_kernel(tab_ref, diag_ref, sub_ref, *, starts):
    h = pl.program_id(0)
    far = tab_ref[REL_BUCKETS - 1, h]
    row = lax.broadcasted_iota(jnp.int32, (ATT_UNIT, ATT_UNIT), 0)
    col = lax.broadcasted_iota(jnp.int32, (ATT_UNIT, ATT_UNIT), 1)
    dist = row - col

    def bias_of(n):
        t = jnp.full(n.shape, (tab_ref[0, h] - far) * LOG2E, F32)
        for b in range(1, REL_BUCKETS):
            t = jnp.where(n >= starts[b], (tab_ref[b, h] - far) * LOG2E, t)
        return t

    diag_ref[0] = jnp.where(dist >= 0, bias_of(dist), NEG)
    sub_ref[0] = bias_of(dist + ATT_UNIT)


def _bias_tiles(rel_bias):
    starts = _bucket_starts()
    assert ATT_UNIT >= starts[-1]
    heads = rel_bias.shape[1]
    tile = pl.BlockSpec((1, ATT_UNIT, ATT_UNIT), lambda h: (h, 0, 0))
    return pl.pallas_call(
        functools.partial(_bias_tiles_kernel, starts=starts),
        grid=(heads,),
        in_specs=[pl.BlockSpec(memory_space=pltpu.SMEM)],
        out_specs=[tile, tile],
        out_shape=[jax.ShapeDtypeStruct((heads, ATT_UNIT, ATT_UNIT), F32)] * 2,
        compiler_params=_cparams("parallel"),
        name="bias_tiles",
    )(rel_bias)


def _attn_kernel(q_ref, k_ref, v_ref, dbias_ref, sbias_ref, lq1_ref, lk1_ref, lq2_ref, lk2_ref,
                 gain_ref, o_ref, m_ref, acc_ref, *, lambda_init):
    i = pl.program_id(1)
    heads = q_ref.shape[1] // (2 * DIFF_QK_DIM)
    q = q_ref[...]
    lane = lax.broadcasted_iota(jnp.int32, (TB, 2 * DIFF_QK_DIM), 1)
    qs = []
    for h in range(heads):
        qh = q[:, h * 2 * DIFF_QK_DIM:(h + 1) * 2 * DIFF_QK_DIM]
        zero = jnp.zeros_like(qh)
        qs.append((jnp.where(lane < DIFF_QK_DIM, qh, zero), jnp.where(lane >= DIFF_QK_DIM, qh, zero)))

    m_ref[...] = jnp.full(m_ref.shape, NEG, F32)
    acc_ref[...] = jnp.zeros(acc_ref.shape, F32)

    def step(j, kind):
        start = pl.multiple_of(j * TB, TB)
        for h in range(heads):
            kj = k_ref[pl.ds(start, TB), h * 2 * DIFF_QK_DIM:(h + 1) * 2 * DIFF_QK_DIM]
            vj = v_ref[pl.ds(start, TB), h * 2 * DIFF_V_DIM:(h + 1) * 2 * DIFF_V_DIM]
            for u in range(TB // ATT_UNIT):
                rows = slice(u * ATT_UNIT, (u + 1) * ATT_UNIT)
                n = (u + 1) * ATT_UNIT if kind == "diag" else TB
                for c in range(2):
                    slot = 2 * h + c
                    s = lax.dot_general(qs[h][c][rows], kj[:n], (((1,), (1,)), ((), ())),
                                        preferred_element_type=F32)
                    cols = [s[:, b * ATT_UNIT:(b + 1) * ATT_UNIT] for b in range(n // ATT_UNIT)]
                    if kind == "diag":
                        cols[u] = cols[u] + dbias_ref[h]
                        if u >= 1:
                            cols[u - 1] = cols[u - 1] + sbias_ref[h]
                    elif kind == "prev" and u == 0:
                        cols[-1] = cols[-1] + sbias_ref[h]
                    m_prev = m_ref[slot, rows, :]
                    m_cur = jnp.max(functools.reduce(jnp.maximum, cols), axis=-1, keepdims=True)
                    m_new = jnp.maximum(m_prev, m_cur)
                    alpha = jnp.exp2(m_prev - m_new)
                    ps = [jnp.exp2(blk - m_new).astype(BF16) for blk in cols]
                    p = ps[0] if len(ps) == 1 else jnp.concatenate(ps, axis=1)
                    pv = jnp.dot(p, vj[:n], preferred_element_type=F32)
                    acc_ref[slot, rows, :] = jnp.concatenate([alpha, alpha], axis=1) * acc_ref[slot, rows, :] + pv
                    m_ref[slot, rows, :] = m_new

    def far(j, carry):
        step(j, "far")
        return carry

    lax.fori_loop(0, jnp.maximum(i - 1, 0), far, 0)

    @pl.when(i >= 1)
    def _():
        step(i - 1, "prev")

    step(i, "diag")

    lam = (jnp.exp(jnp.sum(lq1_ref[...] * lk1_ref[...], axis=-1, keepdims=True))
           - jnp.exp(jnp.sum(lq2_ref[...] * lk2_ref[...], axis=-1, keepdims=True))
           + lambda_init)
    for h in range(heads):
        a1, a2 = acc_ref[2 * h], acc_ref[2 * h + 1]
        o = a1[:, :DIFF_V_DIM] / a1[:, DIFF_V_DIM:] - lam * (a2[:, :DIFF_V_DIM] / a2[:, DIFF_V_DIM:])
        o_ref[:, h * DIFF_V_DIM:(h + 1) * DIFF_V_DIM] = (
            _rms(o, gain_ref[...], SUBLN_EPS) * (1.0 - lambda_init)).astype(BF16)


def _attention(q, k, v, dbias, sbias, lq1, lk1, lq2, lk2, gain, *, batch, lp, lambda_init):
    rows, width = q.shape
    heads = width // (2 * DIFF_QK_DIM)
    nq = lp // TB
    q_spec = pl.BlockSpec((TB, width), lambda b, i: (b * nq + i, 0))
    o_spec = pl.BlockSpec((TB, heads * DIFF_V_DIM), lambda b, i: (b * nq + i, 0))
    k_spec = pl.BlockSpec((lp, width), lambda b, i: (b, 0), pipeline_mode=pl.Buffered(1))
    v_spec = pl.BlockSpec((lp, v.shape[1]), lambda b, i: (b, 0), pipeline_mode=pl.Buffered(1))
    bias_spec = pl.BlockSpec((heads, ATT_UNIT, ATT_UNIT), lambda b, i: (0, 0, 0))
    vec_spec = lambda n: pl.BlockSpec((1, n), lambda b, i: (0, 0))
    return pl.pallas_call(
        functools.partial(_attn_kernel, lambda_init=lambda_init),
        grid=(batch, nq),
        in_specs=[q_spec, k_spec, v_spec, bias_spec, bias_spec,
                  vec_spec(DIFF_QK_DIM), vec_spec(DIFF_QK_DIM), vec_spec(DIFF_QK_DIM),
                  vec_spec(DIFF_QK_DIM), vec_spec(DIFF_V_DIM)],
        out_specs=o_spec,
        out_shape=jax.ShapeDtypeStruct((rows, heads * DIFF_V_DIM), BF16),
        scratch_shapes=[pltpu.VMEM((2 * heads, TB, LANES), F32),
                        pltpu.VMEM((2 * heads, TB, 2 * DIFF_V_DIM), F32)],
        compiler_params=_cparams("parallel", "arbitrary"),
        name="diff_attn",
    )(q, k, v, dbias, sbias, lq1, lk1, lq2, lk2, gain)


def _pool_bands():
    r = np.arange(POOL_SUB)[:, None] + HALO
    c = np.arange(POOL_SUB + HALO)[None, :]
    d = r - c
    return np.stack([((d >= 0) & (d < w)).astype(np.float32) for w in POOL_WINDOWS])


def _merge_kernel(h_ref, u_ref, halo_ref, b_ref, gate_ref, band_ref, gw_ref, scale_ref,
                  wpu_ref, wau_ref, wo_ref, o_ref, *, nt):
    t = pl.program_id(0) % nt
    u = u_ref[...]
    halo = halo_ref[...]
    halo = jnp.where(t > 0, halo, jnp.zeros_like(halo))
    xcat = jnp.concatenate([halo, u], axis=0)
    uf = u.astype(F32)
    pos = t * TB + lax.broadcasted_iota(jnp.int32, (TB, 1), 0)
    d_model = o_ref.shape[1]

    ys = []
    for g, w in enumerate(POOL_WINDOWS):
        lo, hi = g * POOL_GROUP_DIM, (g + 1) * POOL_GROUP_DIM
        cnt = jnp.minimum(pos + 1, w).astype(F32)
        sums = [jnp.dot(band_ref[g], xcat[s * POOL_SUB:s * POOL_SUB + POOL_SUB + HALO, lo:hi],
                        preferred_element_type=F32) for s in range(TB // POOL_SUB)]
        mixed = (jnp.concatenate(sums, axis=0) / cnt - uf[:, lo:hi]).astype(BF16)
        ys.append(jnp.dot(mixed, gw_ref[g], preferred_element_type=F32))
    a_out = (jnp.concatenate(ys, axis=1) * scale_ref[...]).astype(BF16)
    pa = jnp.dot(a_out, wpu_ref[...], preferred_element_type=F32)
    pb = jnp.dot(b_ref[...], wau_ref[...], preferred_element_type=F32)
    gates = gate_ref[...]
    merged = (gates[:, :d_model].astype(F32) * pa + gates[:, d_model:].astype(F32) * pb).astype(BF16)
    o_ref[...] = h_ref[...] + jnp.dot(merged, wo_ref[...], preferred_element_type=F32)


def _merge(h, u, b_out, gates, bands, gw, scale, wpu, wau, wo, *, nt):
    rows, d = h.shape
    pw = u.shape[1]
    halo_per_tile = TB // HALO
    row_blk = lambda c: pl.BlockSpec((TB, c), lambda i: (i, 0))
    full = lambda a: pl.BlockSpec(a.shape, lambda i: (0,) * a.ndim)
    return pl.pallas_call(
        functools.partial(_merge_kernel, nt=nt),
        grid=(rows // TB,),
        in_specs=[row_blk(d), row_blk(pw),
                  pl.BlockSpec((HALO, pw), lambda i: (jnp.maximum(i * halo_per_tile - 1, 0), 0)),
                  row_blk(b_out.shape[1]), row_blk(gates.shape[1]),
                  full(bands), full(gw), full(scale), full(wpu), full(wau), full(wo)],
        out_specs=row_blk(d),
        out_shape=jax.ShapeDtypeStruct((rows, d), F32),
        compiler_params=_cparams("parallel"),
        name="pool_merge",
    )(h, u, u, b_out, gates, bands, gw, scale, wpu, wau, wo)


def _swiglu(x, wg_ref, wu_ref, wd_ref, a_ref, lead=()):
    dff = a_ref.shape[1]
    for lo in range(0, dff, FF_CHUNK):
        cols = lead + (slice(None), slice(lo, lo + FF_CHUNK))
        g = jnp.dot(x, wg_ref[cols], preferred_element_type=F32)
        u = jnp.dot(x, wu_ref[cols], preferred_element_type=F32)
        a_ref[:, lo:lo + FF_CHUNK] = (g * jax.nn.sigmoid(g) * u).astype(BF16)
    return jnp.dot(a_ref[...], wd_ref[lead + (slice(None), slice(None))], preferred_element_type=F32)


def _dense_ffn_kernel(h_ref, g_ref, wg_ref, wu_ref, wd_ref, o_ref, a_ref):
    h = h_ref[...]
    hn = _rms(h, g_ref[...], RMS_EPS).astype(BF16)
    o_ref[...] = h + _swiglu(hn, wg_ref, wu_ref, wd_ref, a_ref)


def _dense_ffn(h, g, wg, wu, wd):
    rows, d = h.shape
    dff = wg.shape[1]
    resident = lambda a: pl.BlockSpec(a.shape, lambda i: (0, 0), pipeline_mode=pl.Buffered(1))
    return pl.pallas_call(
        _dense_ffn_kernel,
        grid=(rows // TB,),
        in_specs=[pl.BlockSpec((TB, d), lambda i: (i, 0)),
                  pl.BlockSpec((1, d), lambda i: (0, 0)),
                  resident(wg), resident(wu), resident(wd)],
        out_specs=pl.BlockSpec((TB, d), lambda i: (i, 0)),
        out_shape=jax.ShapeDtypeStruct((rows, d), F32),
        scratch_shapes=[pltpu.VMEM((TB, dff), BF16)],
        compiler_params=_cparams("parallel"),
        name="dense_ffn",
    )(h, g, wg, wu, wd)


def _split_bf16(x):
    hi = x.astype(BF16)
    return hi, (x - hi.astype(F32)).astype(BF16)


def _router_kernel(h_ref, g_ref, r_ref, hn_ref, idx_ref, wt_ref, *, n_experts):
    hn = _rms(h_ref[...], g_ref[...], RMS_EPS)
    hn_ref[...] = hn
    x_hi, x_lo = _split_bf16(hn)
    r_hi, r_lo = _split_bf16(r_ref[...])
    logits = (jnp.dot(x_hi, r_hi, preferred_element_type=F32)
              + jnp.dot(x_lo, r_hi, preferred_element_type=F32)
              + jnp.dot(x_hi, r_lo, preferred_element_type=F32))
    lane = lax.broadcasted_iota(jnp.int32, logits.shape, 1)
    ninf = jnp.float32(-jnp.inf)
    lg = jnp.where(lane < n_experts, logits, ninf)
    v1 = jnp.max(lg, axis=-1, keepdims=True)
    i1 = jnp.min(jnp.where(lg == v1, lane, LANES), axis=-1, keepdims=True)
    lg2 = jnp.where(lane == i1, ninf, lg)
    v2 = jnp.max(lg2, axis=-1, keepdims=True)
    i2 = jnp.min(jnp.where(lg2 == v2, lane, LANES), axis=-1, keepdims=True)
    e = jnp.exp(v2 - v1)
    w1 = 1.0 / (1.0 + e)
    w2 = e / (1.0 + e)
    idx = jnp.where(lane == 0, i1, jnp.where(lane == 1, i2, 0))
    idx_ref[...] = idx.T[:idx_ref.shape[0], :]
    wt_ref[...] = jnp.where(lane == 0, w1, jnp.where(lane == 1, w2, 0.0))


def _router(h, g, router_padded, n_experts):
    rows, d = h.shape
    row_blk = lambda c: pl.BlockSpec((TB, c), lambda i: (i, 0))
    return pl.pallas_call(
        functools.partial(_router_kernel, n_experts=n_experts),
        grid=(rows // TB,),
        in_specs=[row_blk(d), pl.BlockSpec((1, d), lambda i: (0, 0)),
                  pl.BlockSpec((d, LANES), lambda i: (0, 0))],
        out_specs=[row_blk(d), pl.BlockSpec((SUBLANES, TB), lambda i: (0, i)), row_blk(LANES)],
        out_shape=[jax.ShapeDtypeStruct((rows, d), F32),
                   jax.ShapeDtypeStruct((SUBLANES, rows), jnp.int32),
                   jax.ShapeDtypeStruct((rows, LANES), F32)],
        compiler_params=_cparams("parallel"),
        name="moe_router",
    )(h, g, router_padded)


def _row_copy(src_ref, dst_ref, src_row, dst_row, sem):
    return pltpu.make_async_copy(src_ref.at[pl.ds(src_row, 1), :], dst_ref.at[pl.ds(dst_row, 1), :], sem)


def _for_rows(n_rows, fn):
    def body(r, carry):
        fn(r)
        return carry

    lax.fori_loop(0, n_rows, body, 0, unroll=8)


def _idx_blocks(idx):
    return idx.reshape(idx.shape[0] // GATHER_ROWS, 1, GATHER_ROWS)


_IDX_SPEC = pl.BlockSpec((1, 1, GATHER_ROWS), lambda i, *_: (i, 0, 0), memory_space=pltpu.SMEM)


def _dispatch_kernel(ends_ref, padded_ref, p1_ref, p2_ref, hn_ref, xs_hbm, zero_ref, sem, zero_sem):
    n = hn_ref.shape[0]

    @pl.when(pl.program_id(0) == 0)
    def _():
        zero_ref[...] = jnp.zeros(zero_ref.shape, zero_ref.dtype)
        n_experts = ends_ref.shape[0]
        n_tiles = xs_hbm.shape[0] // MOE_TM

        def clear(first):
            return pltpu.make_async_copy(zero_ref, xs_hbm.at[pl.ds(first, MOE_TM), :], zero_sem)

        def for_each_tile(fn):
            for e in range(n_experts):
                @pl.when(padded_ref[e] > 0)
                def _():
                    fn(clear(pl.multiple_of(ends_ref[e] - MOE_TM, MOE_TM)))

            for t in range(n_tiles - n_experts, n_tiles):
                @pl.when(t * MOE_TM >= ends_ref[n_experts - 1])
                def _():
                    fn(clear(t * MOE_TM))

        for_each_tile(lambda copy: copy.start())
        for_each_tile(lambda copy: copy.wait())

    def start(r):
        _row_copy(hn_ref, xs_hbm, r, p1_ref[0, 0, r], sem.at[0]).start()
        _row_copy(hn_ref, xs_hbm, r, p2_ref[0, 0, r], sem.at[1]).start()

    def wait(r):
        _row_copy(hn_ref, xs_hbm, r, 0, sem.at[0]).wait()
        _row_copy(hn_ref, xs_hbm, r, 0, sem.at[1]).wait()

    _for_rows(n, start)
    _for_rows(n, wait)


def _dispatch(hn, pos1, pos2, ends, padded, sorted_rows):
    rows, d = hn.shape
    grid_spec = pltpu.PrefetchScalarGridSpec(
        num_scalar_prefetch=2,
        grid=(rows // GATHER_ROWS,),
        in_specs=[_IDX_SPEC, _IDX_SPEC, pl.BlockSpec((GATHER_ROWS, d), lambda i, *_: (i, 0))],
        out_specs=pl.BlockSpec(memory_space=pl.ANY),
        scratch_shapes=[pltpu.VMEM((MOE_TM, d), hn.dtype), pltpu.SemaphoreType.DMA((2,)),
                        pltpu.SemaphoreType.DMA(())],
    )
    return pl.pallas_call(
        _dispatch_kernel,
        grid_spec=grid_spec,
        out_shape=jax.ShapeDtypeStruct((sorted_rows, d), hn.dtype),
        compiler_params=_cparams("arbitrary"),
        name="moe_dispatch",
    )(ends, padded, _idx_blocks(pos1), _idx_blocks(pos2), hn)


def _moe_ffn_kernel(te_ref, tv_ref, x_ref, wg_ref, wu_ref, wd_ref, o_ref, a_ref):
    i = pl.program_id(0)

    @pl.when(tv_ref[i] > 0)
    def _():
        o_ref[...] = _swiglu(x_ref[...].astype(BF16), wg_ref, wu_ref, wd_ref, a_ref, lead=(0,))

    @pl.when(tv_ref[i] == 0)
    def _():
        o_ref[...] = jnp.zeros(o_ref.shape, F32)


def _moe_ffn(tile_expert, tile_valid, xs, wg, wu, wd):
    rows, d = xs.shape
    dff = wg.shape[2]
    expert_w = lambda a: pl.BlockSpec((1,) + a.shape[1:], lambda i, te, tv: (te[i], 0, 0),
                                      pipeline_mode=pl.Buffered(1))
    grid_spec = pltpu.PrefetchScalarGridSpec(
        num_scalar_prefetch=2,
        grid=(rows // MOE_TM,),
        in_specs=[pl.BlockSpec((MOE_TM, d), lambda i, te, tv: (i, 0)),
                  expert_w(wg), expert_w(wu), expert_w(wd)],
        out_specs=pl.BlockSpec((MOE_TM, d), lambda i, te, tv: (i, 0)),
        scratch_shapes=[pltpu.VMEM((MOE_TM, dff), BF16)],
    )
    return pl.pallas_call(
        _moe_ffn_kernel,
        grid_spec=grid_spec,
        out_shape=jax.ShapeDtypeStruct((rows, d), F32),
        compiler_params=_cparams("arbitrary"),
        name="moe_ffn",
    )(tile_expert, tile_valid, xs, wg, wu, wd)


def _combine_kernel(p1_ref, p2_ref, h_ref, wt_ref, ys_hbm, g_ref, o_ref, y1_ref, y2_ref, sem, *, final):
    n = o_ref.shape[0]

    def start(r):
        _row_copy(ys_hbm, y1_ref, p1_ref[0, 0, r], r, sem.at[0]).start()
        _row_copy(ys_hbm, y2_ref, p2_ref[0, 0, r], r, sem.at[1]).start()

    def wait(r):
        _row_copy(ys_hbm, y1_ref, 0, r, sem.at[0]).wait()
        _row_copy(ys_hbm, y2_ref, 0, r, sem.at[1]).wait()

    _for_rows(n, start)
    _for_rows(n, wait)
    wt = wt_ref[...]
    out = h_ref[...] + (wt[:, 0:1] * y1_ref[...] + wt[:, 1:2] * y2_ref[...])
    o_ref[...] = _rms(out, g_ref[...], RMS_EPS) if final else out


def _combine(h, wts, ys, pos1, pos2, final_gain):
    rows, d = h.shape
    final = final_gain is not None
    gain = final_gain if final else jnp.ones((1, d), F32)
    row_blk = lambda c: pl.BlockSpec((GATHER_ROWS, c), lambda i: (i, 0))
    return pl.pallas_call(
        functools.partial(_combine_kernel, final=final),
        grid=(rows // GATHER_ROWS,),
        in_specs=[_IDX_SPEC, _IDX_SPEC, row_blk(d), row_blk(LANES),
                  pl.BlockSpec(memory_space=pl.ANY), pl.BlockSpec((1, d), lambda i: (0, 0))],
        out_specs=row_blk(d),
        out_shape=jax.ShapeDtypeStruct((rows, d), F32),
        scratch_shapes=[pltpu.VMEM((GATHER_ROWS, d), F32), pltpu.VMEM((GATHER_ROWS, d), F32),
                        pltpu.SemaphoreType.DMA((2,))],
        compiler_params=_cparams("arbitrary"),
        name="moe_combine",
    )(_idx_blocks(pos1), _idx_blocks(pos2), h, wts, ys, gain)


def _moe_layer(h, g, router, wg, wu, wd, final_gain):
    rows, d = h.shape
    n_experts = router.shape[1]
    router_padded = jnp.pad(router, ((0, 0), (0, LANES - n_experts)))
    hn, idx, wts = _router(h, g, router_padded, n_experts)

    n_assign = rows * TOP_K
    e_flat = idx[:TOP_K].reshape(n_assign)
    onehot = (e_flat[None, :] == jnp.arange(n_experts, dtype=jnp.int32)[:, None]).astype(jnp.int32)
    csum = jnp.cumsum(onehot, axis=1)
    counts = csum[:, -1]
    padded = ((counts + MOE_TM - 1) // MOE_TM) * MOE_TM
    ends = jnp.cumsum(padded)
    pos = jnp.sum(onehot * (csum - 1 + (ends - padded)[:, None]), axis=0)
    n_tiles = n_assign // MOE_TM + n_experts
    tile_start = jnp.arange(n_tiles, dtype=jnp.int32) * MOE_TM
    tile_valid = (tile_start < ends[-1]).astype(jnp.int32)
    tile_expert = jnp.searchsorted(ends, jnp.minimum(tile_start, ends[-1] - 1), side="right").astype(jnp.int32)

    pos1, pos2 = pos[:rows], pos[rows:]
    xs = _dispatch(hn, pos1, pos2, ends.astype(jnp.int32), padded.astype(jnp.int32), n_tiles * MOE_TM)
    ys = _moe_ffn(tile_expert, tile_valid, xs, wg, wu, wd)
    return _combine(h, wts, ys, pos1, pos2, final_gain)


def _final_norm_kernel(h_ref, g_ref, o_ref):
    o_ref[...] = _rms(h_ref[...], g_ref[...], RMS_EPS)


def _final_norm(h, g):
    rows, d = h.shape
    return pl.pallas_call(
        _final_norm_kernel,
        grid=(rows // TB,),
        in_specs=[pl.BlockSpec((TB, d), lambda i: (i, 0)), pl.BlockSpec((1, d), lambda i: (0, 0))],
        out_specs=pl.BlockSpec((TB, d), lambda i: (i, 0)),
        out_shape=jax.ShapeDtypeStruct((rows, d), F32),
        compiler_params=_cparams("parallel"),
        name="final_norm",
    )(h, g)


def kernel(x, meta_tokens, rel_bias, norm_mix, w_in, pool_group_w, pool_scale, lambda_q1, lambda_k1, lambda_q2, lambda_k2, subln_gain, w_pool_up, w_attn_up, w_out, norm_ffn, dense_w_gate, dense_w_up, dense_w_down, moe_router, moe_w_gate, moe_w_up, moe_w_down, final_norm):
    batch, seq, d = x.shape
    depth = w_in.shape[0]
    seq_all = N_META + seq
    lp = -(-seq_all // TB) * TB
    nt = lp // TB
    pw = len(POOL_WINDOWS) * POOL_GROUP_DIM
    qw = DIFF_HEADS * 2 * DIFF_QK_DIM
    assert (batch * lp) % GATHER_ROWS == 0 and TB % HALO == 0 and TB % POOL_SUB == 0

    meta = jnp.broadcast_to(meta_tokens[None].astype(x.dtype), (batch, N_META, d))
    h = jnp.concatenate([meta, x, jnp.zeros((batch, lp - seq_all, d), x.dtype)], axis=1)
    h = h.reshape(batch * lp, d)

    dbias, sbias = _bias_tiles(rel_bias.astype(F32))
    bands = jnp.asarray(_pool_bands(), BF16)
    row = lambda a: a.reshape(1, -1).astype(F32)

    for layer in range(depth):
        u, q, k, v, gates = _inproj(h, row(norm_mix[layer]), w_in[layer].astype(BF16), pw, qw)
        lambda_init = 0.8 - 0.6 * math.exp(-0.3 * layer)
        b_out = _attention(q, k, v, dbias, sbias, row(lambda_q1[layer]), row(lambda_k1[layer]),
                           row(lambda_q2[layer]), row(lambda_k2[layer]), row(subln_gain[layer]),
                           batch=batch, lp=lp, lambda_init=lambda_init)
        h = _merge(h, u, b_out, gates, bands, pool_group_w[layer].astype(BF16), row(pool_scale[layer]),
                   w_pool_up[layer].astype(BF16), w_attn_up[layer].astype(BF16), w_out[layer].astype(BF16),
                   nt=nt)
        j = layer // 2
        if layer % 2 == 0:
            h = _dense_ffn(h, row(norm_ffn[layer]), dense_w_gate[j].astype(BF16),
                           dense_w_up[j].astype(BF16), dense_w_down[j].astype(BF16))
        else:
            h = _moe_layer(h, row(norm_ffn[layer]), moe_router[j].astype(F32), moe_w_gate[j].astype(BF16),
                           moe_w_up[j].astype(BF16), moe_w_down[j].astype(BF16),
                           row(final_norm) if layer == depth - 1 else None)

    out = h if depth % 2 == 0 else _final_norm(h, row(final_norm))
    return out.reshape(batch, lp, d)[:, N_META:seq_all]
```

```python
import functools
import math

import numpy as np
import jax
import jax.numpy as jnp
from jax import lax
from jax.experimental import pallas as pl
from jax.experimental.pallas import tpu as pltpu

F32 = jnp.float32
BF16 = jnp.bfloat16

N_META = 16
POOL_WINDOWS = (2, 4, 8, 16)
POOL_GROUP_DIM = 128
DIFF_HEADS = 4
DIFF_QK_DIM = 64
DIFF_V_DIM = 128
REL_BUCKETS = 32
REL_MAX_DIST = 128
TOP_K = 2
RMS_EPS = 1e-6
SUBLN_EPS = 1e-5

LANES = 128
SUBLANES = 8
TB = 768
ATT_UNIT = 128
HALO = 128
POOL_SUB = 256
FF_CHUNK = 512
MOE_TM = 512
GATHER_ROWS = 512
VMEM_LIMIT = 56 * 1024 * 1024
LOG2E = 1.4426950408889634
NEG = -1e30


def _cparams(*sem):
    return pltpu.CompilerParams(dimension_semantics=sem, vmem_limit_bytes=VMEM_LIMIT)


def _rms(x, g, eps):
    ms = jnp.mean(x * x, axis=-1, keepdims=True)
    return x * lax.rsqrt(ms + eps) * g


def _inproj_kernel(h_ref, g_ref, w_ref, u_ref, q_ref, k_ref, v_ref, gate_ref, *, pw, qw):
    hn = _rms(h_ref[...], g_ref[...], RMS_EPS).astype(BF16)

    def proj(lo, hi):
        return jnp.dot(hn, w_ref[:, lo:hi], preferred_element_type=F32)

    u_ref[...] = proj(0, pw).astype(BF16)
    q_ref[...] = (proj(pw, pw + qw) * (DIFF_QK_DIM ** -0.5 * LOG2E)).astype(BF16)
    k_ref[...] = proj(pw + qw, pw + 2 * qw).astype(BF16)
    v = proj(pw + 2 * qw, pw + 3 * qw).astype(BF16)
    ones = jnp.ones((v.shape[0], DIFF_V_DIM), BF16)
    v_ref[...] = jnp.concatenate(
        [blk for h in range(DIFF_HEADS) for blk in (v[:, h * DIFF_V_DIM:(h + 1) * DIFF_V_DIM], ones)], axis=1)
    gate_ref[...] = jax.nn.sigmoid(proj(pw + 3 * qw, w_ref.shape[1])).astype(BF16)


def _inproj(h, g, w, pw, qw):
    rows, d = h.shape
    cols = w.shape[1]
    gw = cols - pw - 3 * qw
    row_blk = lambda c: pl.BlockSpec((TB, c), lambda i: (i, 0))
    return pl.pallas_call(
        functools.partial(_inproj_kernel, pw=pw, qw=qw),
        grid=(rows // TB,),
        in_specs=[row_blk(d),
                  pl.BlockSpec((1, d), lambda i: (0, 0)),
                  pl.BlockSpec((d, cols), lambda i: (0, 0), pipeline_mode=pl.Buffered(1))],
        out_specs=[row_blk(pw), row_blk(qw), row_blk(qw), row_blk(2 * qw), row_blk(gw)],
        out_shape=[jax.ShapeDtypeStruct((rows, c), BF16) for c in (pw, qw, qw, 2 * qw, gw)],
        compiler_params=_cparams("parallel"),
        name="inproj",
    )(h, g, w)


def _bucket_starts():
    max_exact = REL_BUCKETS // 2
    n = np.arange(1, REL_MAX_DIST + 1)
    large = max_exact + (np.log(n.astype(np.float32) / max_exact) / np.float32(math.log(REL_MAX_DIST / max_exact))
                         * (REL_BUCKETS - max_exact)).astype(np.int32)
    bucket = np.where(n < max_exact, n, np.minimum(large, REL_BUCKETS - 1))
    return [0] + [int(n[bucket >= b].min()) for b in range(1, REL_BUCKETS)]


def _bias_tiles_kernel(tab_ref, diag_ref, sub_ref, *, starts):
    h = pl.program_id(0)
    far = tab_ref[REL_BUCKETS - 1, h]
    row = lax.broadcasted_iota(jnp.int32, (ATT_UNIT, ATT_UNIT), 0)
    col = lax.broadcasted_iota(jnp.int32, (ATT_UNIT, ATT_UNIT), 1)
    dist = row - col

    def bias_of(n):
        t = jnp.full(n.shape, (tab_ref[0, h] - far) * LOG2E, F32)
        for b in range(1, REL_BUCKETS):
            t = jnp.where(n >= starts[b], (tab_ref[b, h] - far) * LOG2E, t)
        return t

    diag_ref[0] = jnp.where(dist >= 0, bias_of(dist), NEG)
    sub_ref[0] = bias_of(dist + ATT_UNIT)


def _bias_tiles(rel_bias):
    starts = _bucket_starts()
    assert ATT_UNIT >= starts[-1]
    heads = rel_bias.shape[1]
    tile = pl.BlockSpec((1, ATT_UNIT, ATT_UNIT), lambda h: (h, 0, 0))
    return pl.pallas_call(
        functools.partial(_bias_tiles_kernel, starts=starts),
        grid=(heads,),
        in_specs=[pl.BlockSpec(memory_space=pltpu.SMEM)],
        out_specs=[tile, tile],
        out_shape=[jax.ShapeDtypeStruct((heads, ATT_UNIT, ATT_UNIT), F32)] * 2,
        compiler_params=_cparams("parallel"),
        name="bias_tiles",
    )(rel_bias)


def _attn_kernel(q_ref, k_ref, v_ref, dbias_ref, sbias_ref, lq1_ref, lk1_ref, lq2_ref, lk2_ref,
                 gain_ref, o_ref, m_ref, acc_ref, *, lambda_init):
    i = pl.program_id(1)
    heads = q_ref.shape[1] // (2 * DIFF_QK_DIM)
    q = q_ref[...]
    lane = lax.broadcasted_iota(jnp.int32, (TB, 2 * DIFF_QK_DIM), 1)
    qs = []
    for h in range(heads):
        qh = q[:, h * 2 * DIFF_QK_DIM:(h + 1) * 2 * DIFF_QK_DIM]
        zero = jnp.zeros_like(qh)
        qs.append((jnp.where(lane < DIFF_QK_DIM, qh, zero), jnp.where(lane >= DIFF_QK_DIM, qh, zero)))

    m_ref[...] = jnp.full(m_ref.shape, NEG, F32)
    acc_ref[...] = jnp.zeros(acc_ref.shape, F32)

    def step(j, kind):
        start = pl.multiple_of(j * TB, TB)
        for h in range(heads):
            kj = k_ref[pl.ds(start, TB), h * 2 * DIFF_QK_DIM:(h + 1) * 2 * DIFF_QK_DIM]
            vj = v_ref[pl.ds(start, TB), h * 2 * DIFF_V_DIM:(h + 1) * 2 * DIFF_V_DIM]
            for u in range(TB // ATT_UNIT):
                rows = slice(u * ATT_UNIT, (u + 1) * ATT_UNIT)
                n = (u + 1) * ATT_UNIT if kind == "diag" else TB
                for c in range(2):
                    slot = 2 * h + c
                    s = lax.dot_general(qs[h][c][rows], kj[:n], (((1,), (1,)), ((), ())),
                                        preferred_element_type=F32)
                    cols = [s[:, b * ATT_UNIT:(b + 1) * ATT_UNIT] for b in range(n // ATT_UNIT)]
                    if kind == "diag":
                        cols[u] = cols[u] + dbias_ref[h]
                        if u >= 1:
                            cols[u - 1] = cols[u - 1] + sbias_ref[h]
                    elif kind == "prev" and u == 0:
                        cols[-1] = cols[-1] + sbias_ref[h]
                    m_prev = m_ref[slot, rows, :]
                    m_cur = jnp.max(functools.reduce(jnp.maximum, cols), axis=-1, keepdims=True)
                    m_new = jnp.maximum(m_prev, m_cur)
                    alpha = jnp.exp2(m_prev - m_new)
                    ps = [jnp.exp2((blk - m_new).astype(BF16)) for blk in cols]
                    p = ps[0] if len(ps) == 1 else jnp.concatenate(ps, axis=1)
                    pv = jnp.dot(p, vj[:n], preferred_element_type=F32)
                    acc_ref[slot, rows, :] = jnp.concatenate([alpha, alpha], axis=1) * acc_ref[slot, rows, :] + pv
                    m_ref[slot, rows, :] = m_new

    def far(j, carry):
        step(j, "far")
        return carry

    lax.fori_loop(0, jnp.maximum(i - 1, 0), far, 0)

    @pl.when(i >= 1)
    def _():
        step(i - 1, "prev")

    step(i, "diag")

    lam = (jnp.exp(jnp.sum(lq1_ref[...] * lk1_ref[...], axis=-1, keepdims=True))
           - jnp.exp(jnp.sum(lq2_ref[...] * lk2_ref[...], axis=-1, keepdims=True))
           + lambda_init)
    for h in range(heads):
        a1, a2 = acc_ref[2 * h], acc_ref[2 * h + 1]
        o = a1[:, :DIFF_V_DIM] / a1[:, DIFF_V_DIM:] - lam * (a2[:, :DIFF_V_DIM] / a2[:, DIFF_V_DIM:])
        o_ref[:, h * DIFF_V_DIM:(h + 1) * DIFF_V_DIM] = (
            _rms(o, gain_ref[...], SUBLN_EPS) * (1.0 - lambda_init)).astype(BF16)


def _attention(q, k, v, dbias, sbias, lq1, lk1, lq2, lk2, gain, *, batch, lp, lambda_init):
    rows, width = q.shape
    heads = width // (2 * DIFF_QK_DIM)
    nq = lp // TB
    q_spec = pl.BlockSpec((TB, width), lambda b, i: (b * nq + i, 0))
    o_spec = pl.BlockSpec((TB, heads * DIFF_V_DIM), lambda b, i: (b * nq + i, 0))
    k_spec = pl.BlockSpec((lp, width), lambda b, i: (b, 0), pipeline_mode=pl.Buffered(1))
    v_spec = pl.BlockSpec((lp, v.shape[1]), lambda b, i: (b, 0), pipeline_mode=pl.Buffered(1))
    bias_spec = pl.BlockSpec((heads, ATT_UNIT, ATT_UNIT), lambda b, i: (0, 0, 0))
    vec_spec = lambda n: pl.BlockSpec((1, n), lambda b, i: (0, 0))
    return pl.pallas_call(
        functools.partial(_attn_kernel, lambda_init=lambda_init),
        grid=(batch, nq),
        in_specs=[q_spec, k_spec, v_spec, bias_spec, bias_spec,
                  vec_spec(DIFF_QK_DIM), vec_spec(DIFF_QK_DIM), vec_spec(DIFF_QK_DIM),
                  vec_spec(DIFF_QK_DIM), vec_spec(DIFF_V_DIM)],
        out_specs=o_spec,
        out_shape=jax.ShapeDtypeStruct((rows, heads * DIFF_V_DIM), BF16),
        scratch_shapes=[pltpu.VMEM((2 * heads, TB, LANES), F32),
                        pltpu.VMEM((2 * heads, TB, 2 * DIFF_V_DIM), F32)],
        compiler_params=_cparams("parallel", "arbitrary"),
        name="diff_attn",
    )(q, k, v, dbias, sbias, lq1, lk1, lq2, lk2, gain)


def _pool_bands():
    r = np.arange(POOL_SUB)[:, None] + HALO
    c = np.arange(POOL_SUB + HALO)[None, :]
    d = r - c
    return np.stack([((d >= 0) & (d < w)).astype(np.float32) for w in POOL_WINDOWS])


def _merge_kernel(h_ref, u_ref, halo_ref, b_ref, gate_ref, band_ref, gw_ref, scale_ref,
                  wpu_ref, wau_ref, wo_ref, o_ref, *, nt):
    t = pl.program_id(0) % nt
    u = u_ref[...]
    halo = halo_ref[...]
    halo = jnp.where(t > 0, halo, jnp.zeros_like(halo))
    xcat = jnp.concatenate([halo, u], axis=0)
    uf = u.astype(F32)
    pos = t * TB + lax.broadcasted_iota(jnp.int32, (TB, 1), 0)
    d_model = o_ref.shape[1]

    ys = []
    for g, w in enumerate(POOL_WINDOWS):
        lo, hi = g * POOL_GROUP_DIM, (g + 1) * POOL_GROUP_DIM
        cnt = jnp.minimum(pos + 1, w).astype(F32)
        sums = [jnp.dot(band_ref[g], xcat[s * POOL_SUB:s * POOL_SUB + POOL_SUB + HALO, lo:hi],
                        preferred_element_type=F32) for s in range(TB // POOL_SUB)]
        mixed = (jnp.concatenate(sums, axis=0) / cnt - uf[:, lo:hi]).astype(BF16)
        ys.append(jnp.dot(mixed, gw_ref[g], preferred_element_type=F32))
    a_out = (jnp.concatenate(ys, axis=1) * scale_ref[...]).astype(BF16)
    pa = jnp.dot(a_out, wpu_ref[...], preferred_element_type=F32)
    pb = jnp.dot(b_ref[...], wau_ref[...], preferred_element_type=F32)
    gates = gate_ref[...]
    merged = (gates[:, :d_model].astype(F32) * pa + gates[:, d_model:].astype(F32) * pb).astype(BF16)
    o_ref[...] = h_ref[...] + jnp.dot(merged, wo_ref[...], preferred_element_type=F32)


def _merge(h, u, b_out, gates, bands, gw, scale, wpu, wau, wo, *, nt):
    rows, d = h.shape
    pw = u.shape[1]
    halo_per_tile = TB // HALO
    row_blk = lambda c: pl.BlockSpec((TB, c), lambda i: (i, 0))
    full = lambda a: pl.BlockSpec(a.shape, lambda i: (0,) * a.ndim)
    return pl.pallas_call(
        functools.partial(_merge_kernel, nt=nt),
        grid=(rows // TB,),
        in_specs=[row_blk(d), row_blk(pw),
                  pl.BlockSpec((HALO, pw), lambda i: (jnp.maximum(i * halo_per_tile - 1, 0), 0)),
                  row_blk(b_out.shape[1]), row_blk(gates.shape[1]),
                  full(bands), full(gw), full(scale), full(wpu), full(wau), full(wo)],
        out_specs=row_blk(d),
        out_shape=jax.ShapeDtypeStruct((rows, d), F32),
        compiler_params=_cparams("parallel"),
        name="pool_merge",
    )(h, u, u, b_out, gates, bands, gw, scale, wpu, wau, wo)


def _swiglu(x, wg_ref, wu_ref, wd_ref, a_ref, lead=()):
    dff = a_ref.shape[1]
    for lo in range(0, dff, FF_CHUNK):
        cols = lead + (slice(None), slice(lo, lo + FF_CHUNK))
        g = jnp.dot(x, wg_ref[cols], preferred_element_type=F32)
        u = jnp.dot(x, wu_ref[cols], preferred_element_type=F32)
        a_ref[:, lo:lo + FF_CHUNK] = (g * jax.nn.sigmoid(g) * u).astype(BF16)
    return jnp.dot(a_ref[...], wd_ref[lead + (slice(None), slice(None))], preferred_element_type=F32)


def _dense_ffn_kernel(h_ref, g_ref, wg_ref, wu_ref, wd_ref, o_ref, a_ref):
    h = h_ref[...]
    hn = _rms(h, g_ref[...], RMS_EPS).astype(BF16)
    o_ref[...] = h + _swiglu(hn, wg_ref, wu_ref, wd_ref, a_ref)


def _dense_ffn(h, g, wg, wu, wd):
    rows, d = h.shape
    dff = wg.shape[1]
    resident = lambda a: pl.BlockSpec(a.shape, lambda i: (0, 0), pipeline_mode=pl.Buffered(1))
    return pl.pallas_call(
        _dense_ffn_kernel,
        grid=(rows // TB,),
        in_specs=[pl.BlockSpec((TB, d), lambda i: (i, 0)),
                  pl.BlockSpec((1, d), lambda i: (0, 0)),
                  resident(wg), resident(wu), resident(wd)],
        out_specs=pl.BlockSpec((TB, d), lambda i: (i, 0)),
        out_shape=jax.ShapeDtypeStruct((rows, d), F32),
        scratch_shapes=[pltpu.VMEM((TB, dff), BF16)],
        compiler_params=_cparams("parallel"),
        name="dense_ffn",
    )(h, g, wg, wu, wd)


def _split_bf16(x):
    hi = x.astype(BF16)
    return hi, (x - hi.astype(F32)).astype(BF16)


def _router_kernel(h_ref, g_ref, r_ref, hn_ref, idx_ref, wt_ref, *, n_experts):
    hn = _rms(h_ref[...], g_ref[...], RMS_EPS)
    hn_ref[...] = hn
    x_hi, x_lo = _split_bf16(hn)
    r_hi, r_lo = _split_bf16(r_ref[...])
    logits = (jnp.dot(x_hi, r_hi, preferred_element_type=F32)
              + jnp.dot(x_lo, r_hi, preferred_element_type=F32)
              + jnp.dot(x_hi, r_lo, preferred_element_type=F32))
    lane = lax.broadcasted_iota(jnp.int32, logits.shape, 1)
    ninf = jnp.float32(-jnp.inf)
    lg = jnp.where(lane < n_experts, logits, ninf)
    v1 = jnp.max(lg, axis=-1, keepdims=True)
    i1 = jnp.min(jnp.where(lg == v1, lane, LANES), axis=-1, keepdims=True)
    lg2 = jnp.where(lane == i1, ninf, lg)
    v2 = jnp.max(lg2, axis=-1, keepdims=True)
    i2 = jnp.min(jnp.where(lg2 == v2, lane, LANES), axis=-1, keepdims=True)
    e = jnp.exp(v2 - v1)
    w1 = 1.0 / (1.0 + e)
    w2 = e / (1.0 + e)
    idx = jnp.where(lane == 0, i1, jnp.where(lane == 1, i2, 0))
    idx_ref[...] = idx.T[:idx_ref.shape[0], :]
    wt_ref[...] = jnp.where(lane == 0, w1, jnp.where(lane == 1, w2, 0.0))


def _router(h, g, router_padded, n_experts):
    rows, d = h.shape
    row_blk = lambda c: pl.BlockSpec((TB, c), lambda i: (i, 0))
    return pl.pallas_call(
        functools.partial(_router_kernel, n_experts=n_experts),
        grid=(rows // TB,),
        in_specs=[row_blk(d), pl.BlockSpec((1, d), lambda i: (0, 0)),
                  pl.BlockSpec((d, LANES), lambda i: (0, 0))],
        out_specs=[row_blk(d), pl.BlockSpec((SUBLANES, TB), lambda i: (0, i)), row_blk(LANES)],
        out_shape=[jax.ShapeDtypeStruct((rows, d), F32),
                   jax.ShapeDtypeStruct((SUBLANES, rows), jnp.int32),
                   jax.ShapeDtypeStruct((rows, LANES), F32)],
        compiler_params=_cparams("parallel"),
        name="moe_router",
    )(h, g, router_padded)


def _hbm_row(ref, row):
    return ref.at[pl.ds(row, 1), :]


def _vmem_row(ref, group, sub):
    return ref.at[pl.ds(pl.multiple_of(group * SUBLANES, SUBLANES), SUBLANES), :].at[pl.ds(sub, 1), :]


def _for_rows(n_rows, fn):
    def body(g, carry):
        for s in range(SUBLANES):
            fn(g, s, g * SUBLANES + s)
        return carry

    lax.fori_loop(0, n_rows // SUBLANES, body, 0)


def _idx_blocks(idx):
    return idx.reshape(idx.shape[0] // GATHER_ROWS, 1, GATHER_ROWS)


_IDX_SPEC = pl.BlockSpec((1, 1, GATHER_ROWS), lambda i, *_: (i, 0, 0), memory_space=pltpu.SMEM)


def _dispatch_kernel(ends_ref, padded_ref, p1_ref, p2_ref, hn_ref, xs_hbm, zero_ref, sem, zero_sem):
    n = hn_ref.shape[0]

    @pl.when(pl.program_id(0) == 0)
    def _():
        zero_ref[...] = jnp.zeros(zero_ref.shape, zero_ref.dtype)
        n_experts = ends_ref.shape[0]
        n_tiles = xs_hbm.shape[0] // MOE_TM

        def clear(first):
            return pltpu.make_async_copy(zero_ref, xs_hbm.at[pl.ds(first, MOE_TM), :], zero_sem)

        def for_each_tile(fn):
            for e in range(n_experts):
                @pl.when(padded_ref[e] > 0)
                def _():
                    fn(clear(pl.multiple_of(ends_ref[e] - MOE_TM, MOE_TM)))

            for t in range(n_tiles - n_experts, n_tiles):
                @pl.when(t * MOE_TM >= ends_ref[n_experts - 1])
                def _():
                    fn(clear(t * MOE_TM))

        for_each_tile(lambda copy: copy.start())
        for_each_tile(lambda copy: copy.wait())

    def copies(g, s, dst1, dst2):
        src = _vmem_row(hn_ref, g, s)
        return (pltpu.make_async_copy(src, _hbm_row(xs_hbm, dst1), sem.at[0]),
                pltpu.make_async_copy(src, _hbm_row(xs_hbm, dst2), sem.at[1]))

    def start(g, s, r):
        for copy in copies(g, s, p1_ref[0, 0, r], p2_ref[0, 0, r]):
            copy.start()

    def wait(g, s, r):
        for copy in copies(g, s, 0, 0):
            copy.wait()

    _for_rows(n, start)
    _for_rows(n, wait)


def _dispatch(hn, pos1, pos2, ends, padded, sorted_rows):
    rows, d = hn.shape
    grid_spec = pltpu.PrefetchScalarGridSpec(
        num_scalar_prefetch=2,
        grid=(rows // GATHER_ROWS,),
        in_specs=[_IDX_SPEC, _IDX_SPEC, pl.BlockSpec((GATHER_ROWS, d), lambda i, *_: (i, 0))],
        out_specs=pl.BlockSpec(memory_space=pl.ANY),
        scratch_shapes=[pltpu.VMEM((MOE_TM, d), hn.dtype), pltpu.SemaphoreType.DMA((2,)),
                        pltpu.SemaphoreType.DMA(())],
    )
    return pl.pallas_call(
        _dispatch_kernel,
        grid_spec=grid_spec,
        out_shape=jax.ShapeDtypeStruct((sorted_rows, d), hn.dtype),
        compiler_params=_cparams("arbitrary"),
        name="moe_dispatch",
    )(ends, padded, _idx_blocks(pos1), _idx_blocks(pos2), hn)


def _moe_ffn_kernel(te_ref, tv_ref, x_ref, wg_ref, wu_ref, wd_ref, o_ref, a_ref):
    i = pl.program_id(0)

    @pl.when(tv_ref[i] > 0)
    def _():
        o_ref[...] = _swiglu(x_ref[...].astype(BF16), wg_ref, wu_ref, wd_ref, a_ref, lead=(0,))

    @pl.when(tv_ref[i] == 0)
    def _():
        o_ref[...] = jnp.zeros(o_ref.shape, F32)


def _moe_ffn(tile_expert, tile_valid, xs, wg, wu, wd):
    rows, d = xs.shape
    dff = wg.shape[2]
    expert_w = lambda a: pl.BlockSpec((1,) + a.shape[1:], lambda i, te, tv: (te[i], 0, 0),
                                      pipeline_mode=pl.Buffered(1))
    grid_spec = pltpu.PrefetchScalarGridSpec(
        num_scalar_prefetch=2,
        grid=(rows // MOE_TM,),
        in_specs=[pl.BlockSpec((MOE_TM, d), lambda i, te, tv: (i, 0)),
                  expert_w(wg), expert_w(wu), expert_w(wd)],
        out_specs=pl.BlockSpec((MOE_TM, d), lambda i, te, tv: (i, 0)),
        scratch_shapes=[pltpu.VMEM((MOE_TM, dff), BF16)],
    )
    return pl.pallas_call(
        _moe_ffn_kernel,
        grid_spec=grid_spec,
        out_shape=jax.ShapeDtypeStruct((rows, d), F32),
        compiler_params=_cparams("arbitrary"),
        name="moe_ffn",
    )(tile_expert, tile_valid, xs, wg, wu, wd)


def _combine_kernel(p1_ref, p2_ref, h_ref, wt_ref, ys_hbm, g_ref, o_ref, y1_ref, y2_ref, sem, *, final):
    n = o_ref.shape[0]

    def copies(g, s, src1, src2):
        return (pltpu.make_async_copy(_hbm_row(ys_hbm, src1), _vmem_row(y1_ref, g, s), sem.at[0]),
                pltpu.make_async_copy(_hbm_row(ys_hbm, src2), _vmem_row(y2_ref, g, s), sem.at[1]))

    def start(g, s, r):
        for copy in copies(g, s, p1_ref[0, 0, r], p2_ref[0, 0, r]):
            copy.start()

    def wait(g, s, r):
        for copy in copies(g, s, 0, 0):
            copy.wait()

    _for_rows(n, start)
    _for_rows(n, wait)
    wt = wt_ref[...]
    out = h_ref[...] + (wt[:, 0:1] * y1_ref[...] + wt[:, 1:2] * y2_ref[...])
    o_ref[...] = _rms(out, g_ref[...], RMS_EPS) if final else out


def _combine(h, wts, ys, pos1, pos2, final_gain):
    rows, d = h.shape
    final = final_gain is not None
    gain = final_gain if final else jnp.ones((1, d), F32)
    row_blk = lambda c: pl.BlockSpec((GATHER_ROWS, c), lambda i: (i, 0))
    return pl.pallas_call(
        functools.partial(_combine_kernel, final=final),
        grid=(rows // GATHER_ROWS,),
        in_specs=[_IDX_SPEC, _IDX_SPEC, row_blk(d), row_blk(LANES),
                  pl.BlockSpec(memory_space=pl.ANY), pl.BlockSpec((1, d), lambda i: (0, 0))],
        out_specs=row_blk(d),
        out_shape=jax.ShapeDtypeStruct((rows, d), F32),
        scratch_shapes=[pltpu.VMEM((GATHER_ROWS, d), F32), pltpu.VMEM((GATHER_ROWS, d), F32),
                        pltpu.SemaphoreType.DMA((2,))],
        compiler_params=_cparams("arbitrary"),
        name="moe_combine",
    )(_idx_blocks(pos1), _idx_blocks(pos2), h, wts, ys, gain)


def _moe_layer(h, g, router, wg, wu, wd, final_gain):
    rows, d = h.shape
    n_experts = router.shape[1]
    router_padded = jnp.pad(router, ((0, 0), (0, LANES - n_experts)))
    hn, idx, wts = _router(h, g, router_padded, n_experts)

    n_assign = rows * TOP_K
    e_flat = idx[:TOP_K].reshape(n_assign)
    onehot = (e_flat[None, :] == jnp.arange(n_experts, dtype=jnp.int32)[:, None]).astype(jnp.int32)
    csum = jnp.cumsum(onehot, axis=1)
    counts = csum[:, -1]
    padded = ((counts + MOE_TM - 1) // MOE_TM) * MOE_TM
    ends = jnp.cumsum(padded)
    pos = jnp.sum(onehot * (csum - 1 + (ends - padded)[:, None]), axis=0)
    n_tiles = n_assign // MOE_TM + n_experts
    tile_start = jnp.arange(n_tiles, dtype=jnp.int32) * MOE_TM
    tile_valid = (tile_start < ends[-1]).astype(jnp.int32)
    tile_expert = jnp.searchsorted(ends, jnp.minimum(tile_start, ends[-1] - 1), side="right").astype(jnp.int32)

    pos1, pos2 = pos[:rows], pos[rows:]
    xs = _dispatch(hn, pos1, pos2, ends.astype(jnp.int32), padded.astype(jnp.int32), n_tiles * MOE_TM)
    ys = _moe_ffn(tile_expert, tile_valid, xs, wg, wu, wd)
    return _combine(h, wts, ys, pos1, pos2, final_gain)


def _final_norm_kernel(h_ref, g_ref, o_ref):
    o_ref[...] = _rms(h_ref[...], g_ref[...], RMS_EPS)


def _final_norm(h, g):
    rows, d = h.shape
    return pl.pallas_call(
        _final_norm_kernel,
        grid=(rows // TB,),
        in_specs=[pl.BlockSpec((TB, d), lambda i: (i, 0)), pl.BlockSpec((1, d), lambda i: (0, 0))],
        out_specs=pl.BlockSpec((TB, d), lambda i: (i, 0)),
        out_shape=jax.ShapeDtypeStruct((rows, d), F32),
        compiler_params=_cparams("parallel"),
        name="final_norm",
    )(h, g)


def kernel(x, meta_tokens, rel_bias, norm_mix, w_in, pool_group_w, pool_scale, lambda_q1, lambda_k1, lambda_q2, lambda_k2, subln_gain, w_pool_up, w_attn_up, w_out, norm_ffn, dense_w_gate, dense_w_up, dense_w_down, moe_router, moe_w_gate, moe_w_up, moe_w_down, final_norm):
    batch, seq, d = x.shape
    depth = w_in.shape[0]
    seq_all = N_META + seq
    lp = -(-seq_all // TB) * TB
    nt = lp // TB
    pw = len(POOL_WINDOWS) * POOL_GROUP_DIM
    qw = DIFF_HEADS * 2 * DIFF_QK_DIM
    assert (batch * lp) % GATHER_ROWS == 0 and TB % HALO == 0 and TB % POOL_SUB == 0

    meta = jnp.broadcast_to(meta_tokens[None].astype(x.dtype), (batch, N_META, d))
    h = jnp.concatenate([meta, x, jnp.zeros((batch, lp - seq_all, d), x.dtype)], axis=1)
    h = h.reshape(batch * lp, d)

    dbias, sbias = _bias_tiles(rel_bias.astype(F32))
    bands = jnp.asarray(_pool_bands(), BF16)
    row = lambda a: a.reshape(1, -1).astype(F32)

    for layer in range(depth):
        u, q, k, v, gates = _inproj(h, row(norm_mix[layer]), w_in[layer].astype(BF16), pw, qw)
        lambda_init = 0.8 - 0.6 * math.exp(-0.3 * layer)
        b_out = _attention(q, k, v, dbias, sbias, row(lambda_q1[layer]), row(lambda_k1[layer]),
                           row(lambda_q2[layer]), row(lambda_k2[layer]), row(subln_gain[layer]),
                           batch=batch, lp=lp, lambda_init=lambda_init)
        h = _merge(h, u, b_out, gates, bands, pool_group_w[layer].astype(BF16), row(pool_scale[layer]),
                   w_pool_up[layer].astype(BF16), w_attn_up[layer].astype(BF16), w_out[layer].astype(BF16),
                   nt=nt)
        j = layer // 2
        if layer % 2 == 0:
            h = _dense_ffn(h, row(norm_ffn[layer]), dense_w_gate[j].astype(BF16),
                           dense_w_up[j].astype(BF16), dense_w_down[j].astype(BF16))
        else:
            h = _moe_layer(h, row(norm_ffn[layer]), moe_router[j].astype(F32), moe_w_gate[j].astype(BF16),
                           moe_w_up[j].astype(BF16), moe_w_down[j].astype(BF16),
                           row(final_norm) if layer == depth - 1 else None)

    out = h if depth % 2 == 0 else _final_norm(h, row(final_norm))
    return out.reshape(batch, lp, d)[:, N_META:seq_all]
```

```python
import functools
import math

import numpy as np
import jax
import jax.numpy as jnp
from jax import lax
from jax.experimental import pallas as pl
from jax.experimental.pallas import tpu as pltpu

F32 = jnp.float32
BF16 = jnp.bfloat16

N_META = 16
POOL_WINDOWS = (2, 4, 8, 16)
POOL_GROUP_DIM = 128
DIFF_HEADS = 4
DIFF_QK_DIM = 64
DIFF_V_DIM = 128
REL_BUCKETS = 32
REL_MAX_DIST = 128
TOP_K = 2
RMS_EPS = 1e-6
SUBLN_EPS = 1e-5

LANES = 128
SUBLANES = 8
TB = 768
ATT_UNIT = 128
HALO = 128
POOL_SUB = 256
FF_CHUNK = 512
MOE_TM = 512
GATHER_ROWS = 512
VMEM_LIMIT = 56 * 1024 * 1024
LOG2E = 1.4426950408889634
NEG = -1e30


def _cparams(*sem):
    return pltpu.CompilerParams(dimension_semantics=sem, vmem_limit_bytes=VMEM_LIMIT)


def _rms(x, g, eps):
    ms = jnp.mean(x * x, axis=-1, keepdims=True)
    return x * lax.rsqrt(ms + eps) * g


def _inproj_kernel(h_ref, g_ref, w_ref, u_ref, q_ref, k_ref, v_ref, gate_ref, *, pw, qw):
    hn = _rms(h_ref[...], g_ref[...], RMS_EPS).astype(BF16)

    def proj(lo, hi):
        return jnp.dot(hn, w_ref[:, lo:hi], preferred_element_type=F32)

    u_ref[...] = proj(0, pw).astype(BF16)
    q_ref[...] = (proj(pw, pw + qw) * (DIFF_QK_DIM ** -0.5 * LOG2E)).astype(BF16)
    k_ref[...] = proj(pw + qw, pw + 2 * qw).astype(BF16)
    v = proj(pw + 2 * qw, pw + 3 * qw).astype(BF16)
    ones = jnp.ones((v.shape[0], DIFF_V_DIM), BF16)
    v_ref[...] = jnp.concatenate(
        [blk for h in range(DIFF_HEADS) for blk in (v[:, h * DIFF_V_DIM:(h + 1) * DIFF_V_DIM], ones)], axis=1)
    gate_ref[...] = jax.nn.sigmoid(proj(pw + 3 * qw, w_ref.shape[1])).astype(BF16)


def _inproj(h, g, w, pw, qw):
    rows, d = h.shape
    cols = w.shape[1]
    gw = cols - pw - 3 * qw
    row_blk = lambda c: pl.BlockSpec((TB, c), lambda i: (i, 0))
    return pl.pallas_call(
        functools.partial(_inproj_kernel, pw=pw, qw=qw),
        grid=(rows // TB,),
        in_specs=[row_blk(d),
                  pl.BlockSpec((1, d), lambda i: (0, 0)),
                  pl.BlockSpec((d, cols), lambda i: (0, 0), pipeline_mode=pl.Buffered(1))],
        out_specs=[row_blk(pw), row_blk(qw), row_blk(qw), row_blk(2 * qw), row_blk(gw)],
        out_shape=[jax.ShapeDtypeStruct((rows, c), BF16) for c in (pw, qw, qw, 2 * qw, gw)],
        compiler_params=_cparams("parallel"),
        name="inproj",
    )(h, g, w)


def _bucket_starts():
    max_exact = REL_BUCKETS // 2
    n = np.arange(1, REL_MAX_DIST + 1)
    large = max_exact + (np.log(n.astype(np.float32) / max_exact) / np.float32(math.log(REL_MAX_DIST / max_exact))
                         * (REL_BUCKETS - max_exact)).astype(np.int32)
    bucket = np.where(n < max_exact, n, np.minimum(large, REL_BUCKETS - 1))
    return [0] + [int(n[bucket >= b].min()) for b in range(1, REL_BUCKETS)]


def _bias_tiles_kernel(tab_ref, diag_ref, sub_ref, *, starts):
    h = pl.program_id(0)
    far = tab_ref[REL_BUCKETS - 1, h]
    row = lax.broadcasted_iota(jnp.int32, (ATT_UNIT, ATT_UNIT), 0)
    col = lax.broadcasted_iota(jnp.int32, (ATT_UNIT, ATT_UNIT), 1)
    dist = row - col

    def bias_of(n):
        t = jnp.full(n.shape, (tab_ref[0, h] - far) * LOG2E, F32)
        for b in range(1, REL_BUCKETS):
            t = jnp.where(n >= starts[b], (tab_ref[b, h] - far) * LOG2E, t)
        return t

    diag_ref[0] = jnp.where(dist >= 0, bias_of(dist), NEG)
    sub_ref[0] = bias_of(dist + ATT_UNIT)


def _bias_tiles(rel_bias):
    starts = _bucket_starts()
    assert ATT_UNIT >= starts[-1]
    heads = rel_bias.shape[1]
    tile = pl.BlockSpec((1, ATT_UNIT, ATT_UNIT), lambda h: (h, 0, 0))
    return pl.pallas_call(
        functools.partial(_bias_tiles_kernel, starts=starts),
        grid=(heads,),
        in_specs=[pl.BlockSpec(memory_space=pltpu.SMEM)],
        out_specs=[tile, tile],
        out_shape=[jax.ShapeDtypeStruct((heads, ATT_UNIT, ATT_UNIT), F32)] * 2,
        compiler_params=_cparams("parallel"),
        name="bias_tiles",
    )(rel_bias)


def _attn_kernel(q_ref, k_ref, v_ref, dbias_ref, sbias_ref, lq1_ref, lk1_ref, lq2_ref, lk2_ref,
                 gain_ref, o_ref, m_ref, acc_ref, *, lambda_init):
    i = pl.program_id(1)
    heads = q_ref.shape[1] // (2 * DIFF_QK_DIM)
    q = q_ref[...]
    lane = lax.broadcasted_iota(jnp.int32, (TB, 2 * DIFF_QK_DIM), 1)
    qs = []
    for h in range(heads):
        qh = q[:, h * 2 * DIFF_QK_DIM:(h + 1) * 2 * DIFF_QK_DIM]
        zero = jnp.zeros_like(qh)
        qs.append((jnp.where(lane < DIFF_QK_DIM, qh, zero), jnp.where(lane >= DIFF_QK_DIM, qh, zero)))

    m_ref[...] = jnp.full(m_ref.shape, NEG, F32)
    acc_ref[...] = jnp.zeros(acc_ref.shape, F32)

    def step(j, kind):
        start = pl.multiple_of(j * TB, TB)
        for h in range(heads):
            kj = k_ref[pl.ds(start, TB), h * 2 * DIFF_QK_DIM:(h + 1) * 2 * DIFF_QK_DIM]
            vj = v_ref[pl.ds(start, TB), h * 2 * DIFF_V_DIM:(h + 1) * 2 * DIFF_V_DIM]
            for u in range(TB // ATT_UNIT):
                rows = slice(u * ATT_UNIT, (u + 1) * ATT_UNIT)
                n = (u + 1) * ATT_UNIT if kind == "diag" else TB
                for c in range(2):
                    slot = 2 * h + c
                    s = lax.dot_general(qs[h][c][rows], kj[:n], (((1,), (1,)), ((), ())),
                                        preferred_element_type=F32)
                    cols = [s[:, b * ATT_UNIT:(b + 1) * ATT_UNIT] for b in range(n // ATT_UNIT)]
                    if kind == "diag":
                        cols[u] = cols[u] + dbias_ref[h]
                        if u >= 1:
                            cols[u - 1] = cols[u - 1] + sbias_ref[h]
                    elif kind == "prev" and u == 0:
                        cols[-1] = cols[-1] + sbias_ref[h]
                    m_prev = m_ref[slot, rows, :]
                    m_cur = jnp.max(functools.reduce(jnp.maximum, cols), axis=-1, keepdims=True)
                    m_new = jnp.maximum(m_prev, m_cur)
                    alpha = jnp.exp2(m_prev - m_new)
                    ps = [jnp.exp2((blk - m_new).astype(BF16)) for blk in cols]
                    p = ps[0] if len(ps) == 1 else jnp.concatenate(ps, axis=1)
                    pv = jnp.dot(p, vj[:n], preferred_element_type=F32)
                    acc_ref[slot, rows, :] = jnp.concatenate([alpha, alpha], axis=1) * acc_ref[slot, rows, :] + pv
                    m_ref[slot, rows, :] = m_new

    def far(j, carry):
        step(j, "far")
        return carry

    lax.fori_loop(0, jnp.maximum(i - 1, 0), far, 0)

    @pl.when(i >= 1)
    def _():
        step(i - 1, "prev")

    step(i, "diag")

    lam = (jnp.exp(jnp.sum(lq1_ref[...] * lk1_ref[...], axis=-1, keepdims=True))
           - jnp.exp(jnp.sum(lq2_ref[...] * lk2_ref[...], axis=-1, keepdims=True))
           + lambda_init)
    for h in range(heads):
        a1, a2 = acc_ref[2 * h], acc_ref[2 * h + 1]
        o = a1[:, :DIFF_V_DIM] / a1[:, DIFF_V_DIM:] - lam * (a2[:, :DIFF_V_DIM] / a2[:, DIFF_V_DIM:])
        o_ref[:, h * DIFF_V_DIM:(h + 1) * DIFF_V_DIM] = (
            _rms(o, gain_ref[...], SUBLN_EPS) * (1.0 - lambda_init)).astype(BF16)


def _attention(q, k, v, dbias, sbias, lq1, lk1, lq2, lk2, gain, *, batch, lp, lambda_init):
    rows, width = q.shape
    heads = width // (2 * DIFF_QK_DIM)
    nq = lp // TB
    q_spec = pl.BlockSpec((TB, width), lambda b, i: (b * nq + i, 0))
    o_spec = pl.BlockSpec((TB, heads * DIFF_V_DIM), lambda b, i: (b * nq + i, 0))
    k_spec = pl.BlockSpec((lp, width), lambda b, i: (b, 0), pipeline_mode=pl.Buffered(1))
    v_spec = pl.BlockSpec((lp, v.shape[1]), lambda b, i: (b, 0), pipeline_mode=pl.Buffered(1))
    bias_spec = pl.BlockSpec((heads, ATT_UNIT, ATT_UNIT), lambda b, i: (0, 0, 0))
    vec_spec = lambda n: pl.BlockSpec((1, n), lambda b, i: (0, 0))
    return pl.pallas_call(
        functools.partial(_attn_kernel, lambda_init=lambda_init),
        grid=(batch, nq),
        in_specs=[q_spec, k_spec, v_spec, bias_spec, bias_spec,
                  vec_spec(DIFF_QK_DIM), vec_spec(DIFF_QK_DIM), vec_spec(DIFF_QK_DIM),
                  vec_spec(DIFF_QK_DIM), vec_spec(DIFF_V_DIM)],
        out_specs=o_spec,
        out_shape=jax.ShapeDtypeStruct((rows, heads * DIFF_V_DIM), BF16),
        scratch_shapes=[pltpu.VMEM((2 * heads, TB, LANES), F32),
                        pltpu.VMEM((2 * heads, TB, 2 * DIFF_V_DIM), F32)],
        compiler_params=_cparams("parallel", "arbitrary"),
        name="diff_attn",
    )(q, k, v, dbias, sbias, lq1, lk1, lq2, lk2, gain)


def _pool_bands():
    r = np.arange(POOL_SUB)[:, None] + HALO
    c = np.arange(POOL_SUB + HALO)[None, :]
    d = r - c
    return np.stack([((d >= 0) & (d < w)).astype(np.float32) for w in POOL_WINDOWS])


def _merge_kernel(h_ref, u_ref, halo_ref, b_ref, gate_ref, band_ref, gw_ref, scale_ref,
                  wpu_ref, wau_ref, wo_ref, o_ref, *, nt):
    t = pl.program_id(0) % nt
    u = u_ref[...]
    halo = halo_ref[...]
    halo = jnp.where(t > 0, halo, jnp.zeros_like(halo))
    xcat = jnp.concatenate([halo, u], axis=0)
    uf = u.astype(F32)
    pos = t * TB + lax.broadcasted_iota(jnp.int32, (TB, 1), 0)
    d_model = o_ref.shape[1]

    ys = []
    for g, w in enumerate(POOL_WINDOWS):
        lo, hi = g * POOL_GROUP_DIM, (g + 1) * POOL_GROUP_DIM
        cnt = jnp.minimum(pos + 1, w).astype(F32)
        sums = [jnp.dot(band_ref[g], xcat[s * POOL_SUB:s * POOL_SUB + POOL_SUB + HALO, lo:hi],
                        preferred_element_type=F32) for s in range(TB // POOL_SUB)]
        mixed = (jnp.concatenate(sums, axis=0) / cnt - uf[:, lo:hi]).astype(BF16)
        ys.append(jnp.dot(mixed, gw_ref[g], preferred_element_type=F32))
    a_out = (jnp.concatenate(ys, axis=1) * scale_ref[...]).astype(BF16)
    pa = jnp.dot(a_out, wpu_ref[...], preferred_element_type=F32)
    pb = jnp.dot(b_ref[...], wau_ref[...], preferred_element_type=F32)
    gates = gate_ref[...]
    merged = (gates[:, :d_model].astype(F32) * pa + gates[:, d_model:].astype(F32) * pb).astype(BF16)
    o_ref[...] = h_ref[...] + jnp.dot(merged, wo_ref[...], preferred_element_type=F32)


def _merge(h, u, b_out, gates, bands, gw, scale, wpu, wau, wo, *, nt):
    rows, d = h.shape
    pw = u.shape[1]
    halo_per_tile = TB // HALO
    row_blk = lambda c: pl.BlockSpec((TB, c), lambda i: (i, 0))
    full = lambda a: pl.BlockSpec(a.shape, lambda i: (0,) * a.ndim)
    return pl.pallas_call(
        functools.partial(_merge_kernel, nt=nt),
        grid=(rows // TB,),
        in_specs=[row_blk(d), row_blk(pw),
                  pl.BlockSpec((HALO, pw), lambda i: (jnp.maximum(i * halo_per_tile - 1, 0), 0)),
                  row_blk(b_out.shape[1]), row_blk(gates.shape[1]),
                  full(bands), full(gw), full(scale), full(wpu), full(wau), full(wo)],
        out_specs=row_blk(d),
        out_shape=jax.ShapeDtypeStruct((rows, d), F32),
        compiler_params=_cparams("parallel"),
        name="pool_merge",
    )(h, u, u, b_out, gates, bands, gw, scale, wpu, wau, wo)


def _swiglu(x, wg_ref, wu_ref, wd_ref, a_ref, lead=()):
    dff = a_ref.shape[1]
    for lo in range(0, dff, FF_CHUNK):
        cols = lead + (slice(None), slice(lo, lo + FF_CHUNK))
        g = jnp.dot(x, wg_ref[cols], preferred_element_type=F32)
        u = jnp.dot(x, wu_ref[cols], preferred_element_type=F32)
        a_ref[:, lo:lo + FF_CHUNK] = (g * jax.nn.sigmoid(g) * u).astype(BF16)
    return jnp.dot(a_ref[...], wd_ref[lead + (slice(None), slice(None))], preferred_element_type=F32)


def _dense_ffn_kernel(h_ref, g_ref, wg_ref, wu_ref, wd_ref, o_ref, a_ref):
    h = h_ref[...]
    hn = _rms(h, g_ref[...], RMS_EPS).astype(BF16)
    o_ref[...] = h + _swiglu(hn, wg_ref, wu_ref, wd_ref, a_ref)


def _dense_ffn(h, g, wg, wu, wd):
    rows, d = h.shape
    dff = wg.shape[1]
    resident = lambda a: pl.BlockSpec(a.shape, lambda i: (0, 0), pipeline_mode=pl.Buffered(1))
    return pl.pallas_call(
        _dense_ffn_kernel,
        grid=(rows // TB,),
        in_specs=[pl.BlockSpec((TB, d), lambda i: (i, 0)),
                  pl.BlockSpec((1, d), lambda i: (0, 0)),
                  resident(wg), resident(wu), resident(wd)],
        out_specs=pl.BlockSpec((TB, d), lambda i: (i, 0)),
        out_shape=jax.ShapeDtypeStruct((rows, d), F32),
        scratch_shapes=[pltpu.VMEM((TB, dff), BF16)],
        compiler_params=_cparams("parallel"),
        name="dense_ffn",
    )(h, g, wg, wu, wd)


def _split_bf16(x):
    hi = x.astype(BF16)
    return hi, (x - hi.astype(F32)).astype(BF16)


def _router_kernel(h_ref, g_ref, r_ref, hn_ref, idx_ref, wt_ref, *, n_experts):
    hn = _rms(h_ref[...], g_ref[...], RMS_EPS)
    hn_ref[...] = hn
    x_hi, x_lo = _split_bf16(hn)
    r_hi, r_lo = _split_bf16(r_ref[...])
    logits = (jnp.dot(x_hi, r_hi, preferred_element_type=F32)
              + jnp.dot(x_lo, r_hi, preferred_element_type=F32)
              + jnp.dot(x_hi, r_lo, preferred_element_type=F32))
    lane = lax.broadcasted_iota(jnp.int32, logits.shape, 1)
    ninf = jnp.float32(-jnp.inf)
    lg = jnp.where(lane < n_experts, logits, ninf)
    v1 = jnp.max(lg, axis=-1, keepdims=True)
    i1 = jnp.min(jnp.where(lg == v1, lane, LANES), axis=-1, keepdims=True)
    lg2 = jnp.where(lane == i1, ninf, lg)
    v2 = jnp.max(lg2, axis=-1, keepdims=True)
    i2 = jnp.min(jnp.where(lg2 == v2, lane, LANES), axis=-1, keepdims=True)
    e = jnp.exp(v2 - v1)
    w1 = 1.0 / (1.0 + e)
    w2 = e / (1.0 + e)
    idx = jnp.where(lane == 0, i1, jnp.where(lane == 1, i2, 0))
    idx_ref[...] = idx.T[:idx_ref.shape[0], :]
    wt_ref[...] = jnp.where(lane == 0, w1, jnp.where(lane == 1, w2, 0.0))


def _router(h, g, router_padded, n_experts):
    rows, d = h.shape
    row_blk = lambda c: pl.BlockSpec((TB, c), lambda i: (i, 0))
    return pl.pallas_call(
        functools.partial(_router_kernel, n_experts=n_experts),
        grid=(rows // TB,),
        in_specs=[row_blk(d), pl.BlockSpec((1, d), lambda i: (0, 0)),
                  pl.BlockSpec((d, LANES), lambda i: (0, 0))],
        out_specs=[row_blk(d), pl.BlockSpec((SUBLANES, TB), lambda i: (0, i)), row_blk(LANES)],
        out_shape=[jax.ShapeDtypeStruct((rows, d), F32),
                   jax.ShapeDtypeStruct((SUBLANES, rows), jnp.int32),
                   jax.ShapeDtypeStruct((rows, LANES), F32)],
        compiler_params=_cparams("parallel"),
        name="moe_router",
    )(h, g, router_padded)


def _hbm_row(ref, row):
    return ref.at[pl.ds(row, 1), :]


def _vmem_row(ref, group, sub):
    return ref.at[pl.ds(pl.multiple_of(group * SUBLANES, SUBLANES), SUBLANES), :].at[pl.ds(sub, 1), :]


def _for_rows(n_rows, fn):
    def body(g, carry):
        for s in range(SUBLANES):
            fn(g, s, g * SUBLANES + s)
        return carry

    lax.fori_loop(0, n_rows // SUBLANES, body, 0)


def _idx_blocks(idx):
    return idx.reshape(idx.shape[0] // GATHER_ROWS, 1, GATHER_ROWS)


_IDX_SPEC = pl.BlockSpec((1, 1, GATHER_ROWS), lambda i, *_: (i, 0, 0), memory_space=pltpu.SMEM)


def _dispatch_kernel(ends_ref, padded_ref, p1_ref, p2_ref, hn_ref, xs_hbm, zero_ref, sem, zero_sem):
    n = hn_ref.shape[0]

    @pl.when(pl.program_id(0) == 0)
    def _():
        zero_ref[...] = jnp.zeros(zero_ref.shape, zero_ref.dtype)
        n_experts = ends_ref.shape[0]
        n_tiles = xs_hbm.shape[0] // MOE_TM

        def clear(first):
            return pltpu.make_async_copy(zero_ref, xs_hbm.at[pl.ds(first, MOE_TM), :], zero_sem)

        def for_each_tile(fn):
            for e in range(n_experts):
                @pl.when(padded_ref[e] > 0)
                def _():
                    fn(clear(pl.multiple_of(ends_ref[e] - MOE_TM, MOE_TM)))

            for t in range(n_tiles - n_experts, n_tiles):
                @pl.when(t * MOE_TM >= ends_ref[n_experts - 1])
                def _():
                    fn(clear(t * MOE_TM))

        for_each_tile(lambda copy: copy.start())
        for_each_tile(lambda copy: copy.wait())

    def copies(g, s, dst1, dst2):
        src = _vmem_row(hn_ref, g, s)
        return (pltpu.make_async_copy(src, _hbm_row(xs_hbm, dst1), sem.at[0]),
                pltpu.make_async_copy(src, _hbm_row(xs_hbm, dst2), sem.at[1]))

    def start(g, s, r):
        for copy in copies(g, s, p1_ref[0, 0, r], p2_ref[0, 0, r]):
            copy.start()

    _for_rows(n, start)
    for k in range(TOP_K):
        pltpu.make_async_copy(hn_ref, xs_hbm.at[pl.ds(0, n), :], sem.at[k]).wait()


def _dispatch(hn, pos1, pos2, ends, padded, sorted_rows):
    rows, d = hn.shape
    grid_spec = pltpu.PrefetchScalarGridSpec(
        num_scalar_prefetch=2,
        grid=(rows // GATHER_ROWS,),
        in_specs=[_IDX_SPEC, _IDX_SPEC, pl.BlockSpec((GATHER_ROWS, d), lambda i, *_: (i, 0))],
        out_specs=pl.BlockSpec(memory_space=pl.ANY),
        scratch_shapes=[pltpu.VMEM((MOE_TM, d), hn.dtype), pltpu.SemaphoreType.DMA((2,)),
                        pltpu.SemaphoreType.DMA(())],
    )
    return pl.pallas_call(
        _dispatch_kernel,
        grid_spec=grid_spec,
        out_shape=jax.ShapeDtypeStruct((sorted_rows, d), hn.dtype),
        compiler_params=_cparams("arbitrary"),
        name="moe_dispatch",
    )(ends, padded, _idx_blocks(pos1), _idx_blocks(pos2), hn)


def _moe_ffn_kernel(te_ref, tv_ref, x_ref, wg_ref, wu_ref, wd_ref, o_ref, a_ref):
    i = pl.program_id(0)

    @pl.when(tv_ref[i] > 0)
    def _():
        o_ref[...] = _swiglu(x_ref[...].astype(BF16), wg_ref, wu_ref, wd_ref, a_ref, lead=(0,))

    @pl.when(tv_ref[i] == 0)
    def _():
        o_ref[...] = jnp.zeros(o_ref.shape, F32)


def _moe_ffn(tile_expert, tile_valid, xs, wg, wu, wd):
    rows, d = xs.shape
    dff = wg.shape[2]
    expert_w = lambda a: pl.BlockSpec((1,) + a.shape[1:], lambda i, te, tv: (te[i], 0, 0),
                                      pipeline_mode=pl.Buffered(1))
    grid_spec = pltpu.PrefetchScalarGridSpec(
        num_scalar_prefetch=2,
        grid=(rows // MOE_TM,),
        in_specs=[pl.BlockSpec((MOE_TM, d), lambda i, te, tv: (i, 0)),
                  expert_w(wg), expert_w(wu), expert_w(wd)],
        out_specs=pl.BlockSpec((MOE_TM, d), lambda i, te, tv: (i, 0)),
        scratch_shapes=[pltpu.VMEM((MOE_TM, dff), BF16)],
    )
    return pl.pallas_call(
        _moe_ffn_kernel,
        grid_spec=grid_spec,
        out_shape=jax.ShapeDtypeStruct((rows, d), F32),
        compiler_params=_cparams("arbitrary"),
        name="moe_ffn",
    )(tile_expert, tile_valid, xs, wg, wu, wd)


def _combine_kernel(p1_ref, p2_ref, h_ref, wt_ref, ys_hbm, g_ref, o_ref, y1_ref, y2_ref, sem, *, final):
    n = o_ref.shape[0]

    def copies(g, s, src1, src2):
        return (pltpu.make_async_copy(_hbm_row(ys_hbm, src1), _vmem_row(y1_ref, g, s), sem.at[0]),
                pltpu.make_async_copy(_hbm_row(ys_hbm, src2), _vmem_row(y2_ref, g, s), sem.at[1]))

    def start(g, s, r):
        for copy in copies(g, s, p1_ref[0, 0, r], p2_ref[0, 0, r]):
            copy.start()

    _for_rows(n, start)
    for k, y_ref in enumerate((y1_ref, y2_ref)):
        pltpu.make_async_copy(ys_hbm.at[pl.ds(0, n), :], y_ref, sem.at[k]).wait()
    wt = wt_ref[...]
    out = h_ref[...] + (wt[:, 0:1] * y1_ref[...] + wt[:, 1:2] * y2_ref[...])
    o_ref[...] = _rms(out, g_ref[...], RMS_EPS) if final else out


def _combine(h, wts, ys, pos1, pos2, final_gain):
    rows, d = h.shape
    final = final_gain is not None
    gain = final_gain if final else jnp.ones((1, d), F32)
    row_blk = lambda c: pl.BlockSpec((GATHER_ROWS, c), lambda i: (i, 0))
    return pl.pallas_call(
        functools.partial(_combine_kernel, final=final),
        grid=(rows // GATHER_ROWS,),
        in_specs=[_IDX_SPEC, _IDX_SPEC, row_blk(d), row_blk(LANES),
                  pl.BlockSpec(memory_space=pl.ANY), pl.BlockSpec((1, d), lambda i: (0, 0))],
        out_specs=row_blk(d),
        out_shape=jax.ShapeDtypeStruct((rows, d), F32),
        scratch_shapes=[pltpu.VMEM((GATHER_ROWS, d), F32), pltpu.VMEM((GATHER_ROWS, d), F32),
                        pltpu.SemaphoreType.DMA((2,))],
        compiler_params=_cparams("arbitrary"),
        name="moe_combine",
    )(_idx_blocks(pos1), _idx_blocks(pos2), h, wts, ys, gain)


def _moe_layer(h, g, router, wg, wu, wd, final_gain):
    rows, d = h.shape
    n_experts = router.shape[1]
    router_padded = jnp.pad(router, ((0, 0), (0, LANES - n_experts)))
    hn, idx, wts = _router(h, g, router_padded, n_experts)

    n_assign = rows * TOP_K
    e_flat = idx[:TOP_K].reshape(n_assign)
    onehot = (e_flat[None, :] == jnp.arange(n_experts, dtype=jnp.int32)[:, None]).astype(jnp.int32)
    csum = jnp.cumsum(onehot, axis=1)
    counts = csum[:, -1]
    padded = ((counts + MOE_TM - 1) // MOE_TM) * MOE_TM
    ends = jnp.cumsum(padded)
    pos = jnp.sum(onehot * (csum - 1 + (ends - padded)[:, None]), axis=0)
    n_tiles = n_assign // MOE_TM + n_experts
    tile_start = jnp.arange(n_tiles, dtype=jnp.int32) * MOE_TM
    tile_valid = (tile_start < ends[-1]).astype(jnp.int32)
    tile_expert = jnp.searchsorted(ends, jnp.minimum(tile_start, ends[-1] - 1), side="right").astype(jnp.int32)

    pos1, pos2 = pos[:rows], pos[rows:]
    xs = _dispatch(hn, pos1, pos2, ends.astype(jnp.int32), padded.astype(jnp.int32), n_tiles * MOE_TM)
    ys = _moe_ffn(tile_expert, tile_valid, xs, wg, wu, wd)
    return _combine(h, wts, ys, pos1, pos2, final_gain)


def _final_norm_kernel(h_ref, g_ref, o_ref):
    o_ref[...] = _rms(h_ref[...], g_ref[...], RMS_EPS)


def _final_norm(h, g):
    rows, d = h.shape
    return pl.pallas_call(
        _final_norm_kernel,
        grid=(rows // TB,),
        in_specs=[pl.BlockSpec((TB, d), lambda i: (i, 0)), pl.BlockSpec((1, d), lambda i: (0, 0))],
        out_specs=pl.BlockSpec((TB, d), lambda i: (i, 0)),
        out_shape=jax.ShapeDtypeStruct((rows, d), F32),
        compiler_params=_cparams("parallel"),
        name="final_norm",
    )(h, g)


def kernel(x, meta_tokens, rel_bias, norm_mix, w_in, pool_group_w, pool_scale, lambda_q1, lambda_k1, lambda_q2, lambda_k2, subln_gain, w_pool_up, w_attn_up, w_out, norm_ffn, dense_w_gate, dense_w_up, dense_w_down, moe_router, moe_w_gate, moe_w_up, moe_w_down, final_norm):
    batch, seq, d = x.shape
    depth = w_in.shape[0]
    seq_all = N_META + seq
    lp = -(-seq_all // TB) * TB
    nt = lp // TB
    pw = len(POOL_WINDOWS) * POOL_GROUP_DIM
    qw = DIFF_HEADS * 2 * DIFF_QK_DIM
    assert (batch * lp) % GATHER_ROWS == 0 and TB % HALO == 0 and TB % POOL_SUB == 0

    meta = jnp.broadcast_to(meta_tokens[None].astype(x.dtype), (batch, N_META, d))
    h = jnp.concatenate([meta, x, jnp.zeros((batch, lp - seq_all, d), x.dtype)], axis=1)
    h = h.reshape(batch * lp, d)

    dbias, sbias = _bias_tiles(rel_bias.astype(F32))
    bands = jnp.asarray(_pool_bands(), BF16)
    row = lambda a: a.reshape(1, -1).astype(F32)

    for layer in range(depth):
        u, q, k, v, gates = _inproj(h, row(norm_mix[layer]), w_in[layer].astype(BF16), pw, qw)
        lambda_init = 0.8 - 0.6 * math.exp(-0.3 * layer)
        b_out = _attention(q, k, v, dbias, sbias, row(lambda_q1[layer]), row(lambda_k1[layer]),
                           row(lambda_q2[layer]), row(lambda_k2[layer]), row(subln_gain[layer]),
                           batch=batch, lp=lp, lambda_init=lambda_init)
        h = _merge(h, u, b_out, gates, bands, pool_group_w[layer].astype(BF16), row(pool_scale[layer]),
                   w_pool_up[layer].astype(BF16), w_attn_up[layer].astype(BF16), w_out[layer].astype(BF16),
                   nt=nt)
        j = layer // 2
        if layer % 2 == 0:
            h = _dense_ffn(h, row(norm_ffn[layer]), dense_w_gate[j].astype(BF16),
                           dense_w_up[j].astype(BF16), dense_w_down[j].astype(BF16))
        else:
            h = _moe_layer(h, row(norm_ffn[layer]), moe_router[j].astype(F32), moe_w_gate[j].astype(BF16),
                           moe_w_up[j].astype(BF16), moe_w_down[j].astype(BF16),
                           row(final_norm) if layer == depth - 1 else None)

    out = h if depth % 2 == 0 else _final_norm(h, row(final_norm))
    return out.reshape(batch, lp, d)[:, N_META:seq_all]
```

```python
import functools
import math

import numpy as np
import jax
import jax.numpy as jnp
from jax import lax
from jax.experimental import pallas as pl
from jax.experimental.pallas import tpu as pltpu

F32 = jnp.float32
BF16 = jnp.bfloat16

N_META = 16
POOL_WINDOWS = (2, 4, 8, 16)
POOL_GROUP_DIM = 128
DIFF_HEADS = 4
DIFF_QK_DIM = 64
DIFF_V_DIM = 128
REL_BUCKETS = 32
REL_MAX_DIST = 128
TOP_K = 2
RMS_EPS = 1e-6
SUBLN_EPS = 1e-5

LANES = 128
SUBLANES = 8
TB = 768
ATT_UNIT = 128
HALO = 128
POOL_SUB = 256
FF_CHUNK = 512
MOE_TM = 512
GATHER_ROWS = 512
VMEM_LIMIT = 56 * 1024 * 1024
LOG2E = 1.4426950408889634
NEG = -1e30


def _cparams(*sem):
    return pltpu.CompilerParams(dimension_semantics=sem, vmem_limit_bytes=VMEM_LIMIT)


def _rms(x, g, eps):
    ms = jnp.mean(x * x, axis=-1, keepdims=True)
    return x * lax.rsqrt(ms + eps) * g


def _inproj_kernel(h_ref, g_ref, w_ref, u_ref, q_ref, k_ref, v_ref, gate_ref, *, pw, qw):
    hn = _rms(h_ref[...], g_ref[...], RMS_EPS).astype(BF16)

    def proj(lo, hi):
        return jnp.dot(hn, w_ref[:, lo:hi], preferred_element_type=F32)

    u_ref[...] = proj(0, pw).astype(BF16)
    q_ref[...] = (proj(pw, pw + qw) * (DIFF_QK_DIM ** -0.5 * LOG2E)).astype(BF16)
    k_ref[...] = proj(pw + qw, pw + 2 * qw).astype(BF16)
    v = proj(pw + 2 * qw, pw + 3 * qw).astype(BF16)
    ones = jnp.ones((v.shape[0], DIFF_V_DIM), BF16)
    v_ref[...] = jnp.concatenate(
        [blk for h in range(DIFF_HEADS) for blk in (v[:, h * DIFF_V_DIM:(h + 1) * DIFF_V_DIM], ones)], axis=1)
    gate_ref[...] = jax.nn.sigmoid(proj(pw + 3 * qw, w_ref.shape[1])).astype(BF16)


def _inproj(h, g, w, pw, qw):
    rows, d = h.shape
    cols = w.shape[1]
    gw = cols - pw - 3 * qw
    row_blk = lambda c: pl.BlockSpec((TB, c), lambda i: (i, 0))
    return pl.pallas_call(
        functools.partial(_inproj_kernel, pw=pw, qw=qw),
        grid=(rows // TB,),
        in_specs=[row_blk(d),
                  pl.BlockSpec((1, d), lambda i: (0, 0)),
                  pl.BlockSpec((d, cols), lambda i: (0, 0), pipeline_mode=pl.Buffered(1))],
        out_specs=[row_blk(pw), row_blk(qw), row_blk(qw), row_blk(2 * qw), row_blk(gw)],
        out_shape=[jax.ShapeDtypeStruct((rows, c), BF16) for c in (pw, qw, qw, 2 * qw, gw)],
        compiler_params=_cparams("parallel"),
        name="inproj",
    )(h, g, w)


def _bucket_starts():
    max_exact = REL_BUCKETS // 2
    n = np.arange(1, REL_MAX_DIST + 1)
    large = max_exact + (np.log(n.astype(np.float32) / max_exact) / np.float32(math.log(REL_MAX_DIST / max_exact))
                         * (REL_BUCKETS - max_exact)).astype(np.int32)
    bucket = np.where(n < max_exact, n, np.minimum(large, REL_BUCKETS - 1))
    return [0] + [int(n[bucket >= b].min()) for b in range(1, REL_BUCKETS)]


def _bias_tiles_kernel(tab_ref, diag_ref, sub_ref, *, starts):
    h = pl.program_id(0)
    far = tab_ref[REL_BUCKETS - 1, h]
    row = lax.broadcasted_iota(jnp.int32, (ATT_UNIT, ATT_UNIT), 0)
    col = lax.broadcasted_iota(jnp.int32, (ATT_UNIT, ATT_UNIT), 1)
    dist = row - col

    def bias_of(n):
        t = jnp.full(n.shape, (tab_ref[0, h] - far) * LOG2E, F32)
        for b in range(1, REL_BUCKETS):
            t = jnp.where(n >= starts[b], (tab_ref[b, h] - far) * LOG2E, t)
        return t

    diag_ref[0] = jnp.where(dist >= 0, bias_of(dist), NEG)
    sub_ref[0] = bias_of(dist + ATT_UNIT)


def _bias_tiles(rel_bias):
    starts = _bucket_starts()
    assert ATT_UNIT >= starts[-1]
    heads = rel_bias.shape[1]
    tile = pl.BlockSpec((1, ATT_UNIT, ATT_UNIT), lambda h: (h, 0, 0))
    return pl.pallas_call(
        functools.partial(_bias_tiles_kernel, starts=starts),
        grid=(heads,),
        in_specs=[pl.BlockSpec(memory_space=pltpu.SMEM)],
        out_specs=[tile, tile],
        out_shape=[jax.ShapeDtypeStruct((heads, ATT_UNIT, ATT_UNIT), F32)] * 2,
        compiler_params=_cparams("parallel"),
        name="bias_tiles",
    )(rel_bias)


def _attn_kernel(q_ref, k_ref, v_ref, dbias_ref, sbias_ref, lq1_ref, lk1_ref, lq2_ref, lk2_ref,
                 gain_ref, o_ref, m_ref, acc_ref, *, lambda_init):
    i = pl.program_id(1)
    heads = q_ref.shape[1] // (2 * DIFF_QK_DIM)
    q = q_ref[...]
    lane = lax.broadcasted_iota(jnp.int32, (TB, 2 * DIFF_QK_DIM), 1)
    qs = []
    for h in range(heads):
        qh = q[:, h * 2 * DIFF_QK_DIM:(h + 1) * 2 * DIFF_QK_DIM]
        zero = jnp.zeros_like(qh)
        qs.append((jnp.where(lane < DIFF_QK_DIM, qh, zero), jnp.where(lane >= DIFF_QK_DIM, qh, zero)))

    m_ref[...] = jnp.full(m_ref.shape, NEG, F32)
    acc_ref[...] = jnp.zeros(acc_ref.shape, F32)

    def step(j, kind):
        start = pl.multiple_of(j * TB, TB)
        for h in range(heads):
            kj = k_ref[pl.ds(start, TB), h * 2 * DIFF_QK_DIM:(h + 1) * 2 * DIFF_QK_DIM]
            vj = v_ref[pl.ds(start, TB), h * 2 * DIFF_V_DIM:(h + 1) * 2 * DIFF_V_DIM]
            for u in range(TB // ATT_UNIT):
                rows = slice(u * ATT_UNIT, (u + 1) * ATT_UNIT)
                n = (u + 1) * ATT_UNIT if kind == "diag" else TB
                for c in range(2):
                    slot = 2 * h + c
                    s = lax.dot_general(qs[h][c][rows], kj[:n], (((1,), (1,)), ((), ())),
                                        preferred_element_type=F32)
                    cols = [s[:, b * ATT_UNIT:(b + 1) * ATT_UNIT] for b in range(n // ATT_UNIT)]
                    if kind == "diag":
                        cols[u] = cols[u] + dbias_ref[h]
                        if u >= 1:
                            cols[u - 1] = cols[u - 1] + sbias_ref[h]
                    elif kind == "prev" and u == 0:
                        cols[-1] = cols[-1] + sbias_ref[h]
                    m_prev = m_ref[slot, rows, :]
                    m_cur = jnp.max(functools.reduce(jnp.maximum, cols), axis=-1, keepdims=True)
                    m_new = jnp.maximum(m_prev, m_cur)
                    alpha = jnp.exp2(m_prev - m_new)
                    ps = [jnp.exp2((blk - m_new).astype(BF16)) for blk in cols]
                    p = ps[0] if len(ps) == 1 else jnp.concatenate(ps, axis=1)
                    pv = jnp.dot(p, vj[:n], preferred_element_type=F32)
                    acc_ref[slot, rows, :] = jnp.concatenate([alpha, alpha], axis=1) * acc_ref[slot, rows, :] + pv
                    m_ref[slot, rows, :] = m_new

    def far(j, carry):
        step(j, "far")
        return carry

    lax.fori_loop(0, jnp.maximum(i - 1, 0), far, 0)

    @pl.when(i >= 1)
    def _():
        step(i - 1, "prev")

    step(i, "diag")

    lam = (jnp.exp(jnp.sum(lq1_ref[...] * lk1_ref[...], axis=-1, keepdims=True))
           - jnp.exp(jnp.sum(lq2_ref[...] * lk2_ref[...], axis=-1, keepdims=True))
           + lambda_init)
    for h in range(heads):
        a1, a2 = acc_ref[2 * h], acc_ref[2 * h + 1]
        o = a1[:, :DIFF_V_DIM] / a1[:, DIFF_V_DIM:] - lam * (a2[:, :DIFF_V_DIM] / a2[:, DIFF_V_DIM:])
        o_ref[:, h * DIFF_V_DIM:(h + 1) * DIFF_V_DIM] = (
            _rms(o, gain_ref[...], SUBLN_EPS) * (1.0 - lambda_init)).astype(BF16)


def _attention(q, k, v, dbias, sbias, lq1, lk1, lq2, lk2, gain, *, batch, lp, lambda_init):
    rows, width = q.shape
    heads = width // (2 * DIFF_QK_DIM)
    nq = lp // TB
    q_spec = pl.BlockSpec((TB, width), lambda b, i: (b * nq + i, 0))
    o_spec = pl.BlockSpec((TB, heads * DIFF_V_DIM), lambda b, i: (b * nq + i, 0))
    k_spec = pl.BlockSpec((lp, width), lambda b, i: (b, 0), pipeline_mode=pl.Buffered(1))
    v_spec = pl.BlockSpec((lp, v.shape[1]), lambda b, i: (b, 0), pipeline_mode=pl.Buffered(1))
    bias_spec = pl.BlockSpec((heads, ATT_UNIT, ATT_UNIT), lambda b, i: (0, 0, 0))
    vec_spec = lambda n: pl.BlockSpec((1, n), lambda b, i: (0, 0))
    return pl.pallas_call(
        functools.partial(_attn_kernel, lambda_init=lambda_init),
        grid=(batch, nq),
        in_specs=[q_spec, k_spec, v_spec, bias_spec, bias_spec,
                  vec_spec(DIFF_QK_DIM), vec_spec(DIFF_QK_DIM), vec_spec(DIFF_QK_DIM),
                  vec_spec(DIFF_QK_DIM), vec_spec(DIFF_V_DIM)],
        out_specs=o_spec,
        out_shape=jax.ShapeDtypeStruct((rows, heads * DIFF_V_DIM), BF16),
        scratch_shapes=[pltpu.VMEM((2 * heads, TB, LANES), F32),
                        pltpu.VMEM((2 * heads, TB, 2 * DIFF_V_DIM), F32)],
        compiler_params=_cparams("parallel", "arbitrary"),
        name="diff_attn",
    )(q, k, v, dbias, sbias, lq1, lk1, lq2, lk2, gain)


def _pool_bands():
    r = np.arange(POOL_SUB)[:, None] + HALO
    c = np.arange(POOL_SUB + HALO)[None, :]
    d = r - c
    return np.stack([((d >= 0) & (d < w)).astype(np.float32) for w in POOL_WINDOWS])


def _merge_kernel(h_ref, u_ref, halo_ref, b_ref, gate_ref, band_ref, gw_ref, scale_ref,
                  wpu_ref, wau_ref, wo_ref, o_ref, *, nt):
    t = pl.program_id(0) % nt
    u = u_ref[...]
    halo = halo_ref[...]
    halo = jnp.where(t > 0, halo, jnp.zeros_like(halo))
    xcat = jnp.concatenate([halo, u], axis=0)
    uf = u.astype(F32)
    pos = t * TB + lax.broadcasted_iota(jnp.int32, (TB, 1), 0)
    d_model = o_ref.shape[1]

    ys = []
    for g, w in enumerate(POOL_WINDOWS):
        lo, hi = g * POOL_GROUP_DIM, (g + 1) * POOL_GROUP_DIM
        cnt = jnp.minimum(pos + 1, w).astype(F32)
        sums = [jnp.dot(band_ref[g], xcat[s * POOL_SUB:s * POOL_SUB + POOL_SUB + HALO, lo:hi],
                        preferred_element_type=F32) for s in range(TB // POOL_SUB)]
        mixed = (jnp.concatenate(sums, axis=0) / cnt - uf[:, lo:hi]).astype(BF16)
        ys.append(jnp.dot(mixed, gw_ref[g], preferred_element_type=F32))
    a_out = (jnp.concatenate(ys, axis=1) * scale_ref[...]).astype(BF16)
    pa = jnp.dot(a_out, wpu_ref[...], preferred_element_type=F32)
    pb = jnp.dot(b_ref[...], wau_ref[...], preferred_element_type=F32)
    gates = gate_ref[...]
    merged = (gates[:, :d_model].astype(F32) * pa + gates[:, d_model:].astype(F32) * pb).astype(BF16)
    o_ref[...] = h_ref[...] + jnp.dot(merged, wo_ref[...], preferred_element_type=F32)


def _merge(h, u, b_out, gates, bands, gw, scale, wpu, wau, wo, *, nt):
    rows, d = h.shape
    pw = u.shape[1]
    halo_per_tile = TB // HALO
    row_blk = lambda c: pl.BlockSpec((TB, c), lambda i: (i, 0))
    full = lambda a: pl.BlockSpec(a.shape, lambda i: (0,) * a.ndim)
    return pl.pallas_call(
        functools.partial(_merge_kernel, nt=nt),
        grid=(rows // TB,),
        in_specs=[row_blk(d), row_blk(pw),
                  pl.BlockSpec((HALO, pw), lambda i: (jnp.maximum(i * halo_per_tile - 1, 0), 0)),
                  row_blk(b_out.shape[1]), row_blk(gates.shape[1]),
                  full(bands), full(gw), full(scale), full(wpu), full(wau), full(wo)],
        out_specs=row_blk(d),
        out_shape=jax.ShapeDtypeStruct((rows, d), F32),
        compiler_params=_cparams("parallel"),
        name="pool_merge",
    )(h, u, u, b_out, gates, bands, gw, scale, wpu, wau, wo)


def _swiglu(x, wg_ref, wu_ref, wd_ref, a_ref, lead=()):
    dff = a_ref.shape[1]
    for lo in range(0, dff, FF_CHUNK):
        cols = lead + (slice(None), slice(lo, lo + FF_CHUNK))
        g = jnp.dot(x, wg_ref[cols], preferred_element_type=F32)
        u = jnp.dot(x, wu_ref[cols], preferred_element_type=F32)
        a_ref[:, lo:lo + FF_CHUNK] = (g * jax.nn.sigmoid(g) * u).astype(BF16)
    return jnp.dot(a_ref[...], wd_ref[lead + (slice(None), slice(None))], preferred_element_type=F32)


def _dense_ffn_kernel(h_ref, g_ref, wg_ref, wu_ref, wd_ref, o_ref, a_ref):
    h = h_ref[...]
    hn = _rms(h, g_ref[...], RMS_EPS).astype(BF16)
    o_ref[...] = h + _swiglu(hn, wg_ref, wu_ref, wd_ref, a_ref)


def _dense_ffn(h, g, wg, wu, wd):
    rows, d = h.shape
    dff = wg.shape[1]
    resident = lambda a: pl.BlockSpec(a.shape, lambda i: (0, 0), pipeline_mode=pl.Buffered(1))
    return pl.pallas_call(
        _dense_ffn_kernel,
        grid=(rows // TB,),
        in_specs=[pl.BlockSpec((TB, d), lambda i: (i, 0)),
                  pl.BlockSpec((1, d), lambda i: (0, 0)),
                  resident(wg), resident(wu), resident(wd)],
        out_specs=pl.BlockSpec((TB, d), lambda i: (i, 0)),
        out_shape=jax.ShapeDtypeStruct((rows, d), F32),
        scratch_shapes=[pltpu.VMEM((TB, dff), BF16)],
        compiler_params=_cparams("parallel"),
        name="dense_ffn",
    )(h, g, wg, wu, wd)


def _split_bf16(x):
    hi = x.astype(BF16)
    return hi, (x - hi.astype(F32)).astype(BF16)


def _router_kernel(h_ref, g_ref, r_ref, hn_ref, idx_ref, wt_ref, *, n_experts):
    hn = _rms(h_ref[...], g_ref[...], RMS_EPS)
    hn_ref[...] = hn
    x_hi, x_lo = _split_bf16(hn)
    r_hi, r_lo = _split_bf16(r_ref[...])
    logits = (jnp.dot(x_hi, r_hi, preferred_element_type=F32)
              + jnp.dot(x_lo, r_hi, preferred_element_type=F32)
              + jnp.dot(x_hi, r_lo, preferred_element_type=F32))
    lane = lax.broadcasted_iota(jnp.int32, logits.shape, 1)
    ninf = jnp.float32(-jnp.inf)
    lg = jnp.where(lane < n_experts, logits, ninf)
    v1 = jnp.max(lg, axis=-1, keepdims=True)
    i1 = jnp.min(jnp.where(lg == v1, lane, LANES), axis=-1, keepdims=True)
    lg2 = jnp.where(lane == i1, ninf, lg)
    v2 = jnp.max(lg2, axis=-1, keepdims=True)
    i2 = jnp.min(jnp.where(lg2 == v2, lane, LANES), axis=-1, keepdims=True)
    e = jnp.exp(v2 - v1)
    w1 = 1.0 / (1.0 + e)
    w2 = e / (1.0 + e)
    idx = jnp.where(lane == 0, i1, jnp.where(lane == 1, i2, 0))
    idx_ref[...] = idx.T[:idx_ref.shape[0], :]
    wt_ref[...] = jnp.where(lane == 0, w1, jnp.where(lane == 1, w2, 0.0))


def _router(h, g, router_padded, n_experts):
    rows, d = h.shape
    row_blk = lambda c: pl.BlockSpec((TB, c), lambda i: (i, 0))
    return pl.pallas_call(
        functools.partial(_router_kernel, n_experts=n_experts),
        grid=(rows // TB,),
        in_specs=[row_blk(d), pl.BlockSpec((1, d), lambda i: (0, 0)),
                  pl.BlockSpec((d, LANES), lambda i: (0, 0))],
        out_specs=[row_blk(d), pl.BlockSpec((SUBLANES, TB), lambda i: (0, i)), row_blk(LANES)],
        out_shape=[jax.ShapeDtypeStruct((rows, d), F32),
                   jax.ShapeDtypeStruct((SUBLANES, rows), jnp.int32),
                   jax.ShapeDtypeStruct((rows, LANES), F32)],
        compiler_params=_cparams("parallel"),
        name="moe_router",
    )(h, g, router_padded)


def _hbm_row(ref, row):
    return ref.at[pl.ds(row, 1), :]


def _vmem_row(ref, group, sub):
    return ref.at[pl.ds(pl.multiple_of(group * SUBLANES, SUBLANES), SUBLANES), :].at[pl.ds(sub, 1), :]


def _for_rows(n_rows, fn):
    def body(g, carry):
        for s in range(SUBLANES):
            fn(g, s, g * SUBLANES + s)
        return carry

    lax.fori_loop(0, n_rows // SUBLANES, body, 0)


def _idx_blocks(idx):
    return idx.reshape(idx.shape[0] // GATHER_ROWS, 1, GATHER_ROWS)


_IDX_SPEC = pl.BlockSpec((1, 1, GATHER_ROWS), lambda i, *_: (i, 0, 0), memory_space=pltpu.SMEM)


def _dispatch_kernel(ends_ref, padded_ref, p1_ref, p2_ref, hn_ref, xs_hbm, zero_ref, sem, zero_sem):
    n = hn_ref.shape[0]

    @pl.when(pl.program_id(0) == 0)
    def _():
        zero_ref[...] = jnp.zeros(zero_ref.shape, zero_ref.dtype)
        n_experts = ends_ref.shape[0]
        n_tiles = xs_hbm.shape[0] // MOE_TM

        def clear(first):
            return pltpu.make_async_copy(zero_ref, xs_hbm.at[pl.ds(first, MOE_TM), :], zero_sem)

        def for_each_tile(fn):
            for e in range(n_experts):
                @pl.when(padded_ref[e] > 0)
                def _():
                    fn(clear(pl.multiple_of(ends_ref[e] - MOE_TM, MOE_TM)))

            for t in range(n_tiles - n_experts, n_tiles):
                @pl.when(t * MOE_TM >= ends_ref[n_experts - 1])
                def _():
                    fn(clear(t * MOE_TM))

        for_each_tile(lambda copy: copy.start())
        for_each_tile(lambda copy: copy.wait())

    def copies(g, s, dst1, dst2):
        src = _vmem_row(hn_ref, g, s)
        return (pltpu.make_async_copy(src, _hbm_row(xs_hbm, dst1), sem.at[0]),
                pltpu.make_async_copy(src, _hbm_row(xs_hbm, dst2), sem.at[1]))

    def start(g, s, r):
        for copy in copies(g, s, p1_ref[0, 0, r], p2_ref[0, 0, r]):
            copy.start()

    _for_rows(n, start)
    for k in range(TOP_K):
        pltpu.make_async_copy(hn_ref, xs_hbm.at[pl.ds(0, n), :], sem.at[k]).wait()


def _dispatch(hn, pos1, pos2, ends, padded, sorted_rows):
    rows, d = hn.shape
    grid_spec = pltpu.PrefetchScalarGridSpec(
        num_scalar_prefetch=2,
        grid=(rows // GATHER_ROWS,),
        in_specs=[_IDX_SPEC, _IDX_SPEC, pl.BlockSpec((GATHER_ROWS, d), lambda i, *_: (i, 0))],
        out_specs=pl.BlockSpec(memory_space=pl.ANY),
        scratch_shapes=[pltpu.VMEM((MOE_TM, d), hn.dtype), pltpu.SemaphoreType.DMA((2,)),
                        pltpu.SemaphoreType.DMA(())],
    )
    return pl.pallas_call(
        _dispatch_kernel,
        grid_spec=grid_spec,
        out_shape=jax.ShapeDtypeStruct((sorted_rows, d), hn.dtype),
        compiler_params=_cparams("arbitrary"),
        name="moe_dispatch",
    )(ends, padded, _idx_blocks(pos1), _idx_blocks(pos2), hn)


def _moe_ffn_kernel(te_ref, tv_ref, x_ref, wg_ref, wu_ref, wd_ref, o_ref, a_ref):
    i = pl.program_id(0)

    @pl.when(tv_ref[i] > 0)
    def _():
        o_ref[...] = _swiglu(x_ref[...].astype(BF16), wg_ref, wu_ref, wd_ref, a_ref, lead=(0,))

    @pl.when(tv_ref[i] == 0)
    def _():
        o_ref[...] = jnp.zeros(o_ref.shape, F32)


def _moe_ffn(tile_expert, tile_valid, xs, wg, wu, wd):
    rows, d = xs.shape
    dff = wg.shape[2]
    expert_w = lambda a: pl.BlockSpec((1,) + a.shape[1:], lambda i, te, tv: (te[i], 0, 0),
                                      pipeline_mode=pl.Buffered(1))
    grid_spec = pltpu.PrefetchScalarGridSpec(
        num_scalar_prefetch=2,
        grid=(rows // MOE_TM,),
        in_specs=[pl.BlockSpec((MOE_TM, d), lambda i, te, tv: (i, 0)),
                  expert_w(wg), expert_w(wu), expert_w(wd)],
        out_specs=pl.BlockSpec((MOE_TM, d), lambda i, te, tv: (i, 0)),
        scratch_shapes=[pltpu.VMEM((MOE_TM, dff), BF16)],
    )
    return pl.pallas_call(
        _moe_ffn_kernel,
        grid_spec=grid_spec,
        out_shape=jax.ShapeDtypeStruct((rows, d), F32),
        compiler_params=_cparams("arbitrary"),
        name="moe_ffn",
    )(tile_expert, tile_valid, xs, wg, wu, wd)


def _combine_kernel(p1_ref, p2_ref, q1_ref, q2_ref, h_ref, wt_ref, ys_hbm, g_ref, o_ref, y_ref, sem, *, final):
    n = o_ref.shape[0]
    i = pl.program_id(0)
    slot = i % 2

    def gather(idx_refs, into):
        def start(g, s, r):
            for k in range(TOP_K):
                pltpu.make_async_copy(_hbm_row(ys_hbm, idx_refs[k][0, 0, r]),
                                      _vmem_row(y_ref.at[into, k], g, s), sem.at[into, k]).start()

        _for_rows(n, start)

    @pl.when(i == 0)
    def _():
        gather((p1_ref, p2_ref), 0)

    @pl.when(i + 1 < pl.num_programs(0))
    def _():
        gather((q1_ref, q2_ref), 1 - slot)

    for k in range(TOP_K):
        pltpu.make_async_copy(ys_hbm.at[pl.ds(0, n), :], y_ref.at[slot, k], sem.at[slot, k]).wait()
    wt = wt_ref[...]
    out = h_ref[...] + (wt[:, 0:1] * y_ref[slot, 0] + wt[:, 1:2] * y_ref[slot, 1])
    o_ref[...] = _rms(out, g_ref[...], RMS_EPS) if final else out


def _combine(h, wts, ys, pos1, pos2, final_gain):
    rows, d = h.shape
    final = final_gain is not None
    gain = final_gain if final else jnp.ones((1, d), F32)
    row_blk = lambda c: pl.BlockSpec((GATHER_ROWS, c), lambda i: (i, 0))
    n_blocks = rows // GATHER_ROWS
    next_idx = pl.BlockSpec((1, 1, GATHER_ROWS), lambda i: (jnp.minimum(i + 1, n_blocks - 1), 0, 0),
                            memory_space=pltpu.SMEM)
    idx1, idx2 = _idx_blocks(pos1), _idx_blocks(pos2)
    return pl.pallas_call(
        functools.partial(_combine_kernel, final=final),
        grid=(n_blocks,),
        in_specs=[_IDX_SPEC, _IDX_SPEC, next_idx, next_idx, row_blk(d), row_blk(LANES),
                  pl.BlockSpec(memory_space=pl.ANY), pl.BlockSpec((1, d), lambda i: (0, 0))],
        out_specs=row_blk(d),
        out_shape=jax.ShapeDtypeStruct((rows, d), F32),
        scratch_shapes=[pltpu.VMEM((2, TOP_K, GATHER_ROWS, d), F32), pltpu.SemaphoreType.DMA((2, TOP_K))],
        compiler_params=_cparams("arbitrary"),
        name="moe_combine",
    )(idx1, idx2, idx1, idx2, h, wts, ys, gain)


def _moe_layer(h, g, router, wg, wu, wd, final_gain):
    rows, d = h.shape
    n_experts = router.shape[1]
    router_padded = jnp.pad(router, ((0, 0), (0, LANES - n_experts)))
    hn, idx, wts = _router(h, g, router_padded, n_experts)

    n_assign = rows * TOP_K
    e_flat = idx[:TOP_K].reshape(n_assign)
    onehot = (e_flat[None, :] == jnp.arange(n_experts, dtype=jnp.int32)[:, None]).astype(jnp.int32)
    csum = jnp.cumsum(onehot, axis=1)
    counts = csum[:, -1]
    padded = ((counts + MOE_TM - 1) // MOE_TM) * MOE_TM
    ends = jnp.cumsum(padded)
    pos = jnp.sum(onehot * (csum - 1 + (ends - padded)[:, None]), axis=0)
    n_tiles = n_assign // MOE_TM + n_experts
    tile_start = jnp.arange(n_tiles, dtype=jnp.int32) * MOE_TM
    tile_valid = (tile_start < ends[-1]).astype(jnp.int32)
    tile_expert = jnp.searchsorted(ends, jnp.minimum(tile_start, ends[-1] - 1), side="right").astype(jnp.int32)

    pos1, pos2 = pos[:rows], pos[rows:]
    xs = _dispatch(hn, pos1, pos2, ends.astype(jnp.int32), padded.astype(jnp.int32), n_tiles * MOE_TM)
    ys = _moe_ffn(tile_expert, tile_valid, xs, wg, wu, wd)
    return _combine(h, wts, ys, pos1, pos2, final_gain)


def _final_norm_kernel(h_ref, g_ref, o_ref):
    o_ref[...] = _rms(h_ref[...], g_ref[...], RMS_EPS)


def _final_norm(h, g):
    rows, d = h.shape
    return pl.pallas_call(
        _final_norm_kernel,
        grid=(rows // TB,),
        in_specs=[pl.BlockSpec((TB, d), lambda i: (i, 0)), pl.BlockSpec((1, d), lambda i: (0, 0))],
        out_specs=pl.BlockSpec((TB, d), lambda i: (i, 0)),
        out_shape=jax.ShapeDtypeStruct((rows, d), F32),
        compiler_params=_cparams("parallel"),
        name="final_norm",
    )(h, g)


def kernel(x, meta_tokens, rel_bias, norm_mix, w_in, pool_group_w, pool_scale, lambda_q1, lambda_k1, lambda_q2, lambda_k2, subln_gain, w_pool_up, w_attn_up, w_out, norm_ffn, dense_w_gate, dense_w_up, dense_w_down, moe_router, moe_w_gate, moe_w_up, moe_w_down, final_norm):
    batch, seq, d = x.shape
    depth = w_in.shape[0]
    seq_all = N_META + seq
    lp = -(-seq_all // TB) * TB
    nt = lp // TB
    pw = len(POOL_WINDOWS) * POOL_GROUP_DIM
    qw = DIFF_HEADS * 2 * DIFF_QK_DIM
    assert (batch * lp) % GATHER_ROWS == 0 and TB % HALO == 0 and TB % POOL_SUB == 0

    meta = jnp.broadcast_to(meta_tokens[None].astype(x.dtype), (batch, N_META, d))
    h = jnp.concatenate([meta, x, jnp.zeros((batch, lp - seq_all, d), x.dtype)], axis=1)
    h = h.reshape(batch * lp, d)

    dbias, sbias = _bias_tiles(rel_bias.astype(F32))
    bands = jnp.asarray(_pool_bands(), BF16)
    row = lambda a: a.reshape(1, -1).astype(F32)

    for layer in range(depth):
        u, q, k, v, gates = _inproj(h, row(norm_mix[layer]), w_in[layer].astype(BF16), pw, qw)
        lambda_init = 0.8 - 0.6 * math.exp(-0.3 * layer)
        b_out = _attention(q, k, v, dbias, sbias, row(lambda_q1[layer]), row(lambda_k1[layer]),
                           row(lambda_q2[layer]), row(lambda_k2[layer]), row(subln_gain[layer]),
                           batch=batch, lp=lp, lambda_init=lambda_init)
        h = _merge(h, u, b_out, gates, bands, pool_group_w[layer].astype(BF16), row(pool_scale[layer]),
                   w_pool_up[layer].astype(BF16), w_attn_up[layer].astype(BF16), w_out[layer].astype(BF16),
                   nt=nt)
        j = layer // 2
        if layer % 2 == 0:
            h = _dense_ffn(h, row(norm_ffn[layer]), dense_w_gate[j].astype(BF16),
                           dense_w_up[j].astype(BF16), dense_w_down[j].astype(BF16))
        else:
            h = _moe_layer(h, row(norm_ffn[layer]), moe_router[j].astype(F32), moe_w_gate[j].astype(BF16),
                           moe_w_up[j].astype(BF16), moe_w_down[j].astype(BF16),
                           row(final_norm) if layer == depth - 1 else None)

    out = h if depth % 2 == 0 else _final_norm(h, row(final_norm))
    return out.reshape(batch, lp, d)[:, N_META:seq_all]
```

```python
import functools
import math

import numpy as np
import jax
import jax.numpy as jnp
from jax import lax
from jax.experimental import pallas as pl
from jax.experimental.pallas import tpu as pltpu

F32 = jnp.float32
BF16 = jnp.bfloat16

N_META = 16
POOL_WINDOWS = (2, 4, 8, 16)
POOL_GROUP_DIM = 128
DIFF_HEADS = 4
DIFF_QK_DIM = 64
DIFF_V_DIM = 128
REL_BUCKETS = 32
REL_MAX_DIST = 128
TOP_K = 2
RMS_EPS = 1e-6
SUBLN_EPS = 1e-5

LANES = 128
SUBLANES = 8
TB = 768
ATT_UNIT = 128
HALO = 128
POOL_SUB = 256
FF_CHUNK = 512
MOE_TM = 512
GATHER_ROWS = 512
VMEM_LIMIT = 56 * 1024 * 1024
LOG2E = 1.4426950408889634
NEG = -1e30


def _cparams(*sem):
    return pltpu.CompilerParams(dimension_semantics=sem, vmem_limit_bytes=VMEM_LIMIT)


def _rms(x, g, eps):
    ms = jnp.mean(x * x, axis=-1, keepdims=True)
    return x * lax.rsqrt(ms + eps) * g


def _inproj_kernel(h_ref, g_ref, w_ref, u_ref, q_ref, k_ref, v_ref, gate_ref, *, pw, qw):
    hn = _rms(h_ref[...], g_ref[...], RMS_EPS).astype(BF16)

    def proj(lo, hi):
        return jnp.dot(hn, w_ref[:, lo:hi], preferred_element_type=F32)

    u_ref[...] = proj(0, pw).astype(BF16)
    q_ref[...] = (proj(pw, pw + qw) * (DIFF_QK_DIM ** -0.5 * LOG2E)).astype(BF16)
    k_ref[...] = proj(pw + qw, pw + 2 * qw).astype(BF16)
    v = proj(pw + 2 * qw, pw + 3 * qw).astype(BF16)
    ones = jnp.ones((v.shape[0], DIFF_V_DIM), BF16)
    v_ref[...] = jnp.concatenate(
        [blk for h in range(DIFF_HEADS) for blk in (v[:, h * DIFF_V_DIM:(h + 1) * DIFF_V_DIM], ones)], axis=1)
    gate_ref[...] = jax.nn.sigmoid(proj(pw + 3 * qw, w_ref.shape[1])).astype(BF16)


def _inproj(h, g, w, pw, qw):
    rows, d = h.shape
    cols = w.shape[1]
    gw = cols - pw - 3 * qw
    row_blk = lambda c: pl.BlockSpec((TB, c), lambda i: (i, 0))
    return pl.pallas_call(
        functools.partial(_inproj_kernel, pw=pw, qw=qw),
        grid=(rows // TB,),
        in_specs=[row_blk(d),
                  pl.BlockSpec((1, d), lambda i: (0, 0)),
                  pl.BlockSpec((d, cols), lambda i: (0, 0), pipeline_mode=pl.Buffered(1))],
        out_specs=[row_blk(pw), row_blk(qw), row_blk(qw), row_blk(2 * qw), row_blk(gw)],
        out_shape=[jax.ShapeDtypeStruct((rows, c), BF16) for c in (pw, qw, qw, 2 * qw, gw)],
        compiler_params=_cparams("parallel"),
        name="inproj",
    )(h, g, w)


def _bucket_starts():
    max_exact = REL_BUCKETS // 2
    n = np.arange(1, REL_MAX_DIST + 1)
    large = max_exact + (np.log(n.astype(np.float32) / max_exact) / np.float32(math.log(REL_MAX_DIST / max_exact))
                         * (REL_BUCKETS - max_exact)).astype(np.int32)
    bucket = np.where(n < max_exact, n, np.minimum(large, REL_BUCKETS - 1))
    return [0] + [int(n[bucket >= b].min()) for b in range(1, REL_BUCKETS)]


def _bias_tiles_kernel(tab_ref, diag_ref, sub_ref, *, starts):
    h = pl.program_id(0)
    far = tab_ref[REL_BUCKETS - 1, h]
    row = lax.broadcasted_iota(jnp.int32, (ATT_UNIT, ATT_UNIT), 0)
    col = lax.broadcasted_iota(jnp.int32, (ATT_UNIT, ATT_UNIT), 1)
    dist = row - col

    def bias_of(n):
        t = jnp.full(n.shape, (tab_ref[0, h] - far) * LOG2E, F32)
        for b in range(1, REL_BUCKETS):
            t = jnp.where(n >= starts[b], (tab_ref[b, h] - far) * LOG2E, t)
        return t

    diag_ref[0] = jnp.where(dist >= 0, bias_of(dist), NEG)
    sub_ref[0] = bias_of(dist + ATT_UNIT)


def _bias_tiles(rel_bias):
    starts = _bucket_starts()
    assert ATT_UNIT >= starts[-1]
    heads = rel_bias.shape[1]
    tile = pl.BlockSpec((1, ATT_UNIT, ATT_UNIT), lambda h: (h, 0, 0))
    return pl.pallas_call(
        functools.partial(_bias_tiles_kernel, starts=starts),
        grid=(heads,),
        in_specs=[pl.BlockSpec(memory_space=pltpu.SMEM)],
        out_specs=[tile, tile],
        out_shape=[jax.ShapeDtypeStruct((heads, ATT_UNIT, ATT_UNIT), F32)] * 2,
        compiler_params=_cparams("parallel"),
        name="bias_tiles",
    )(rel_bias)


def _attn_kernel(q_ref, k_ref, v_ref, dbias_ref, sbias_ref, lq1_ref, lk1_ref, lq2_ref, lk2_ref,
                 gain_ref, o_ref, m_ref, acc_ref, *, lambda_init):
    i = pl.program_id(1)
    heads = q_ref.shape[1] // (2 * DIFF_QK_DIM)
    q = q_ref[...]
    lane = lax.broadcasted_iota(jnp.int32, (TB, 2 * DIFF_QK_DIM), 1)
    qs = []
    for h in range(heads):
        qh = q[:, h * 2 * DIFF_QK_DIM:(h + 1) * 2 * DIFF_QK_DIM]
        zero = jnp.zeros_like(qh)
        qs.append((jnp.where(lane < DIFF_QK_DIM, qh, zero), jnp.where(lane >= DIFF_QK_DIM, qh, zero)))

    m_ref[...] = jnp.full(m_ref.shape, NEG, F32)
    acc_ref[...] = jnp.zeros(acc_ref.shape, F32)

    def step(j, kind):
        start = pl.multiple_of(j * TB, TB)
        for h in range(heads):
            kj = k_ref[pl.ds(start, TB), h * 2 * DIFF_QK_DIM:(h + 1) * 2 * DIFF_QK_DIM]
            vj = v_ref[pl.ds(start, TB), h * 2 * DIFF_V_DIM:(h + 1) * 2 * DIFF_V_DIM]
            for u in range(TB // ATT_UNIT):
                rows = slice(u * ATT_UNIT, (u + 1) * ATT_UNIT)
                n = (u + 1) * ATT_UNIT if kind == "diag" else TB
                for c in range(2):
                    slot = 2 * h + c
                    s = lax.dot_general(qs[h][c][rows], kj[:n], (((1,), (1,)), ((), ())),
                                        preferred_element_type=F32)
                    cols = [s[:, b * ATT_UNIT:(b + 1) * ATT_UNIT] for b in range(n // ATT_UNIT)]
                    if kind == "diag":
                        cols[u] = cols[u] + dbias_ref[h]
                        if u >= 1:
                            cols[u - 1] = cols[u - 1] + sbias_ref[h]
                    elif kind == "prev" and u == 0:
                        cols[-1] = cols[-1] + sbias_ref[h]
                    m_prev = m_ref[slot, rows, :]
                    m_cur = jnp.max(functools.reduce(jnp.maximum, cols), axis=-1, keepdims=True)
                    m_new = jnp.maximum(m_prev, m_cur)
                    alpha = jnp.exp2(m_prev - m_new)
                    ps = [jnp.exp2((blk - m_new).astype(BF16)) for blk in cols]
                    p = ps[0] if len(ps) == 1 else jnp.concatenate(ps, axis=1)
                    pv = jnp.dot(p, vj[:n], preferred_element_type=F32)
                    acc_ref[slot, rows, :] = jnp.concatenate([alpha, alpha], axis=1) * acc_ref[slot, rows, :] + pv
                    m_ref[slot, rows, :] = m_new

    def far(j, carry):
        step(j, "far")
        return carry

    lax.fori_loop(0, jnp.maximum(i - 1, 0), far, 0)

    @pl.when(i >= 1)
    def _():
        step(i - 1, "prev")

    step(i, "diag")

    lam = (jnp.exp(jnp.sum(lq1_ref[...] * lk1_ref[...], axis=-1, keepdims=True))
           - jnp.exp(jnp.sum(lq2_ref[...] * lk2_ref[...], axis=-1, keepdims=True))
           + lambda_init)
    for h in range(heads):
        a1, a2 = acc_ref[2 * h], acc_ref[2 * h + 1]
        o = a1[:, :DIFF_V_DIM] / a1[:, DIFF_V_DIM:] - lam * (a2[:, :DIFF_V_DIM] / a2[:, DIFF_V_DIM:])
        o_ref[:, h * DIFF_V_DIM:(h + 1) * DIFF_V_DIM] = (
            _rms(o, gain_ref[...], SUBLN_EPS) * (1.0 - lambda_init)).astype(BF16)


def _attention(q, k, v, dbias, sbias, lq1, lk1, lq2, lk2, gain, *, batch, lp, lambda_init):
    rows, width = q.shape
    heads = width // (2 * DIFF_QK_DIM)
    nq = lp // TB
    q_spec = pl.BlockSpec((TB, width), lambda b, i: (b * nq + i, 0))
    o_spec = pl.BlockSpec((TB, heads * DIFF_V_DIM), lambda b, i: (b * nq + i, 0))
    k_spec = pl.BlockSpec((lp, width), lambda b, i: (b, 0), pipeline_mode=pl.Buffered(1))
    v_spec = pl.BlockSpec((lp, v.shape[1]), lambda b, i: (b, 0), pipeline_mode=pl.Buffered(1))
    bias_spec = pl.BlockSpec((heads, ATT_UNIT, ATT_UNIT), lambda b, i: (0, 0, 0))
    vec_spec = lambda n: pl.BlockSpec((1, n), lambda b, i: (0, 0))
    return pl.pallas_call(
        functools.partial(_attn_kernel, lambda_init=lambda_init),
        grid=(batch, nq),
        in_specs=[q_spec, k_spec, v_spec, bias_spec, bias_spec,
                  vec_spec(DIFF_QK_DIM), vec_spec(DIFF_QK_DIM), vec_spec(DIFF_QK_DIM),
                  vec_spec(DIFF_QK_DIM), vec_spec(DIFF_V_DIM)],
        out_specs=o_spec,
        out_shape=jax.ShapeDtypeStruct((rows, heads * DIFF_V_DIM), BF16),
        scratch_shapes=[pltpu.VMEM((2 * heads, TB, LANES), F32),
                        pltpu.VMEM((2 * heads, TB, 2 * DIFF_V_DIM), F32)],
        compiler_params=_cparams("parallel", "arbitrary"),
        name="diff_attn",
    )(q, k, v, dbias, sbias, lq1, lk1, lq2, lk2, gain)


def _pool_bands():
    r = np.arange(POOL_SUB)[:, None] + HALO
    c = np.arange(POOL_SUB + HALO)[None, :]
    d = r - c
    return np.stack([((d >= 0) & (d < w)).astype(np.float32) for w in POOL_WINDOWS])


def _merge_kernel(h_ref, u_ref, halo_ref, b_ref, gate_ref, band_ref, gw_ref, scale_ref,
                  wpu_ref, wau_ref, wo_ref, o_ref, *, nt):
    t = pl.program_id(0) % nt
    u = u_ref[...]
    halo = halo_ref[...]
    halo = jnp.where(t > 0, halo, jnp.zeros_like(halo))
    xcat = jnp.concatenate([halo, u], axis=0)
    uf = u.astype(F32)
    pos = t * TB + lax.broadcasted_iota(jnp.int32, (TB, 1), 0)
    d_model = o_ref.shape[1]

    ys = []
    for g, w in enumerate(POOL_WINDOWS):
        lo, hi = g * POOL_GROUP_DIM, (g + 1) * POOL_GROUP_DIM
        cnt = jnp.minimum(pos + 1, w).astype(F32)
        sums = [jnp.dot(band_ref[g], xcat[s * POOL_SUB:s * POOL_SUB + POOL_SUB + HALO, lo:hi],
                        preferred_element_type=F32) for s in range(TB // POOL_SUB)]
        mixed = (jnp.concatenate(sums, axis=0) / cnt - uf[:, lo:hi]).astype(BF16)
        ys.append(jnp.dot(mixed, gw_ref[g], preferred_element_type=F32))
    a_out = (jnp.concatenate(ys, axis=1) * scale_ref[...]).astype(BF16)
    pa = jnp.dot(a_out, wpu_ref[...], preferred_element_type=F32)
    pb = jnp.dot(b_ref[...], wau_ref[...], preferred_element_type=F32)
    gates = gate_ref[...]
    merged = (gates[:, :d_model].astype(F32) * pa + gates[:, d_model:].astype(F32) * pb).astype(BF16)
    o_ref[...] = h_ref[...] + jnp.dot(merged, wo_ref[...], preferred_element_type=F32)


def _merge(h, u, b_out, gates, bands, gw, scale, wpu, wau, wo, *, nt):
    rows, d = h.shape
    pw = u.shape[1]
    halo_per_tile = TB // HALO
    row_blk = lambda c: pl.BlockSpec((TB, c), lambda i: (i, 0))
    full = lambda a: pl.BlockSpec(a.shape, lambda i: (0,) * a.ndim)
    return pl.pallas_call(
        functools.partial(_merge_kernel, nt=nt),
        grid=(rows // TB,),
        in_specs=[row_blk(d), row_blk(pw),
                  pl.BlockSpec((HALO, pw), lambda i: (jnp.maximum(i * halo_per_tile - 1, 0), 0)),
                  row_blk(b_out.shape[1]), row_blk(gates.shape[1]),
                  full(bands), full(gw), full(scale), full(wpu), full(wau), full(wo)],
        out_specs=row_blk(d),
        out_shape=jax.ShapeDtypeStruct((rows, d), F32),
        compiler_params=_cparams("parallel"),
        name="pool_merge",
    )(h, u, u, b_out, gates, bands, gw, scale, wpu, wau, wo)


def _swiglu(x, wg_ref, wu_ref, wd_ref, a_ref, lead=()):
    dff = a_ref.shape[1]
    for lo in range(0, dff, FF_CHUNK):
        cols = lead + (slice(None), slice(lo, lo + FF_CHUNK))
        g = jnp.dot(x, wg_ref[cols], preferred_element_type=F32)
        u = jnp.dot(x, wu_ref[cols], preferred_element_type=F32)
        a_ref[:, lo:lo + FF_CHUNK] = (g * jax.nn.sigmoid(g) * u).astype(BF16)
    return jnp.dot(a_ref[...], wd_ref[lead + (slice(None), slice(None))], preferred_element_type=F32)


def _dense_ffn_kernel(h_ref, g_ref, wg_ref, wu_ref, wd_ref, o_ref, a_ref):
    h = h_ref[...]
    hn = _rms(h, g_ref[...], RMS_EPS).astype(BF16)
    o_ref[...] = h + _swiglu(hn, wg_ref, wu_ref, wd_ref, a_ref)


def _dense_ffn(h, g, wg, wu, wd):
    rows, d = h.shape
    dff = wg.shape[1]
    resident = lambda a: pl.BlockSpec(a.shape, lambda i: (0, 0), pipeline_mode=pl.Buffered(1))
    return pl.pallas_call(
        _dense_ffn_kernel,
        grid=(rows // TB,),
        in_specs=[pl.BlockSpec((TB, d), lambda i: (i, 0)),
                  pl.BlockSpec((1, d), lambda i: (0, 0)),
                  resident(wg), resident(wu), resident(wd)],
        out_specs=pl.BlockSpec((TB, d), lambda i: (i, 0)),
        out_shape=jax.ShapeDtypeStruct((rows, d), F32),
        scratch_shapes=[pltpu.VMEM((TB, dff), BF16)],
        compiler_params=_cparams("parallel"),
        name="dense_ffn",
    )(h, g, wg, wu, wd)


def _split_bf16(x):
    hi = x.astype(BF16)
    return hi, (x - hi.astype(F32)).astype(BF16)


def _router_kernel(h_ref, g_ref, r_ref, hn_ref, idx_ref, wt_ref, *, n_experts):
    hn = _rms(h_ref[...], g_ref[...], RMS_EPS)
    hn_ref[...] = hn
    x_hi, x_lo = _split_bf16(hn)
    r_hi, r_lo = _split_bf16(r_ref[...])
    logits = (jnp.dot(x_hi, r_hi, preferred_element_type=F32)
              + jnp.dot(x_lo, r_hi, preferred_element_type=F32)
              + jnp.dot(x_hi, r_lo, preferred_element_type=F32))
    lane = lax.broadcasted_iota(jnp.int32, logits.shape, 1)
    ninf = jnp.float32(-jnp.inf)
    lg = jnp.where(lane < n_experts, logits, ninf)
    v1 = jnp.max(lg, axis=-1, keepdims=True)
    i1 = jnp.min(jnp.where(lg == v1, lane, LANES), axis=-1, keepdims=True)
    lg2 = jnp.where(lane == i1, ninf, lg)
    v2 = jnp.max(lg2, axis=-1, keepdims=True)
    i2 = jnp.min(jnp.where(lg2 == v2, lane, LANES), axis=-1, keepdims=True)
    e = jnp.exp(v2 - v1)
    w1 = 1.0 / (1.0 + e)
    w2 = e / (1.0 + e)
    idx = jnp.where(lane == 0, i1, jnp.where(lane == 1, i2, 0))
    idx_ref[...] = idx.T[:idx_ref.shape[0], :]
    wt_ref[...] = jnp.where(lane == 0, w1, jnp.where(lane == 1, w2, 0.0))


def _router(h, g, router_padded, n_experts):
    rows, d = h.shape
    row_blk = lambda c: pl.BlockSpec((TB, c), lambda i: (i, 0))
    return pl.pallas_call(
        functools.partial(_router_kernel, n_experts=n_experts),
        grid=(rows // TB,),
        in_specs=[row_blk(d), pl.BlockSpec((1, d), lambda i: (0, 0)),
                  pl.BlockSpec((d, LANES), lambda i: (0, 0))],
        out_specs=[row_blk(d), pl.BlockSpec((SUBLANES, TB), lambda i: (0, i)), row_blk(LANES)],
        out_shape=[jax.ShapeDtypeStruct((rows, d), F32),
                   jax.ShapeDtypeStruct((SUBLANES, rows), jnp.int32),
                   jax.ShapeDtypeStruct((rows, LANES), F32)],
        compiler_params=_cparams("parallel"),
        name="moe_router",
    )(h, g, router_padded)


def _hbm_row(ref, row):
    return ref.at[pl.ds(row, 1), :]


def _vmem_row(ref, group, sub):
    return ref.at[pl.ds(pl.multiple_of(group * SUBLANES, SUBLANES), SUBLANES), :].at[pl.ds(sub, 1), :]


def _for_rows(n_rows, fn):
    def body(g, carry):
        for s in range(SUBLANES):
            fn(g, s, g * SUBLANES + s)
        return carry

    lax.fori_loop(0, n_rows // SUBLANES, body, 0)


def _idx_blocks(idx):
    return idx.reshape(idx.shape[0] // GATHER_ROWS, 1, GATHER_ROWS)


_IDX_SPEC = pl.BlockSpec((1, 1, GATHER_ROWS), lambda i, *_: (i, 0, 0), memory_space=pltpu.SMEM)


def _dispatch_kernel(ends_ref, padded_ref, p1_ref, p2_ref, hn_ref, xs_hbm, zero_ref, sem, zero_sem):
    n = hn_ref.shape[0]

    @pl.when(pl.program_id(0) == 0)
    def _():
        zero_ref[...] = jnp.zeros(zero_ref.shape, zero_ref.dtype)
        n_experts = ends_ref.shape[0]
        n_tiles = xs_hbm.shape[0] // MOE_TM

        def clear(first):
            return pltpu.make_async_copy(zero_ref, xs_hbm.at[pl.ds(first, MOE_TM), :], zero_sem)

        def for_each_tile(fn):
            for e in range(n_experts):
                @pl.when(padded_ref[e] > 0)
                def _():
                    fn(clear(pl.multiple_of(ends_ref[e] - MOE_TM, MOE_TM)))

            for t in range(n_tiles - n_experts, n_tiles):
                @pl.when(t * MOE_TM >= ends_ref[n_experts - 1])
                def _():
                    fn(clear(t * MOE_TM))

        for_each_tile(lambda copy: copy.start())
        for_each_tile(lambda copy: copy.wait())

    def copies(g, s, dst1, dst2):
        src = _vmem_row(hn_ref, g, s)
        return (pltpu.make_async_copy(src, _hbm_row(xs_hbm, dst1), sem.at[0]),
                pltpu.make_async_copy(src, _hbm_row(xs_hbm, dst2), sem.at[1]))

    def start(g, s, r):
        for k, copy in enumerate(copies(g, s, p1_ref[0, 0, r], p2_ref[0, 0, r])):
            copy.start(priority=k)

    _for_rows(n, start)
    for k in range(TOP_K):
        pltpu.make_async_copy(hn_ref, xs_hbm.at[pl.ds(0, n), :], sem.at[k]).wait()


def _dispatch(hn, pos1, pos2, ends, padded, sorted_rows):
    rows, d = hn.shape
    grid_spec = pltpu.PrefetchScalarGridSpec(
        num_scalar_prefetch=2,
        grid=(rows // GATHER_ROWS,),
        in_specs=[_IDX_SPEC, _IDX_SPEC, pl.BlockSpec((GATHER_ROWS, d), lambda i, *_: (i, 0))],
        out_specs=pl.BlockSpec(memory_space=pl.ANY),
        scratch_shapes=[pltpu.VMEM((MOE_TM, d), hn.dtype), pltpu.SemaphoreType.DMA((2,)),
                        pltpu.SemaphoreType.DMA(())],
    )
    return pl.pallas_call(
        _dispatch_kernel,
        grid_spec=grid_spec,
        out_shape=jax.ShapeDtypeStruct((sorted_rows, d), hn.dtype),
        compiler_params=_cparams("arbitrary"),
        name="moe_dispatch",
    )(ends, padded, _idx_blocks(pos1), _idx_blocks(pos2), hn)


def _moe_ffn_kernel(te_ref, tv_ref, x_ref, wg_ref, wu_ref, wd_ref, o_ref, a_ref):
    i = pl.program_id(0)

    @pl.when(tv_ref[i] > 0)
    def _():
        o_ref[...] = _swiglu(x_ref[...].astype(BF16), wg_ref, wu_ref, wd_ref, a_ref, lead=(0,))

    @pl.when(tv_ref[i] == 0)
    def _():
        o_ref[...] = jnp.zeros(o_ref.shape, F32)


def _moe_ffn(tile_expert, tile_valid, xs, wg, wu, wd):
    rows, d = xs.shape
    dff = wg.shape[2]
    expert_w = lambda a: pl.BlockSpec((1,) + a.shape[1:], lambda i, te, tv: (te[i], 0, 0),
                                      pipeline_mode=pl.Buffered(1))
    grid_spec = pltpu.PrefetchScalarGridSpec(
        num_scalar_prefetch=2,
        grid=(rows // MOE_TM,),
        in_specs=[pl.BlockSpec((MOE_TM, d), lambda i, te, tv: (i, 0)),
                  expert_w(wg), expert_w(wu), expert_w(wd)],
        out_specs=pl.BlockSpec((MOE_TM, d), lambda i, te, tv: (i, 0)),
        scratch_shapes=[pltpu.VMEM((MOE_TM, dff), BF16)],
    )
    return pl.pallas_call(
        _moe_ffn_kernel,
        grid_spec=grid_spec,
        out_shape=jax.ShapeDtypeStruct((rows, d), F32),
        compiler_params=_cparams("arbitrary"),
        name="moe_ffn",
    )(tile_expert, tile_valid, xs, wg, wu, wd)


def _combine_kernel(p1_ref, p2_ref, q1_ref, q2_ref, h_ref, wt_ref, ys_hbm, g_ref, o_ref, y_ref, sem, *, final):
    n = o_ref.shape[0]
    i = pl.program_id(0)
    slot = i % 2

    def gather(idx_refs, into):
        def start(g, s, r):
            for k in range(TOP_K):
                pltpu.make_async_copy(_hbm_row(ys_hbm, idx_refs[k][0, 0, r]),
                                      _vmem_row(y_ref.at[into, k], g, s), sem.at[into, k]).start(priority=k)

        _for_rows(n, start)

    @pl.when(i == 0)
    def _():
        gather((p1_ref, p2_ref), 0)

    @pl.when(i + 1 < pl.num_programs(0))
    def _():
        gather((q1_ref, q2_ref), 1 - slot)

    for k in range(TOP_K):
        pltpu.make_async_copy(ys_hbm.at[pl.ds(0, n), :], y_ref.at[slot, k], sem.at[slot, k]).wait()
    wt = wt_ref[...]
    out = h_ref[...] + (wt[:, 0:1] * y_ref[slot, 0] + wt[:, 1:2] * y_ref[slot, 1])
    o_ref[...] = _rms(out, g_ref[...], RMS_EPS) if final else out


def _combine(h, wts, ys, pos1, pos2, final_gain):
    rows, d = h.shape
    final = final_gain is not None
    gain = final_gain if final else jnp.ones((1, d), F32)
    row_blk = lambda c: pl.BlockSpec((GATHER_ROWS, c), lambda i: (i, 0))
    n_blocks = rows // GATHER_ROWS
    next_idx = pl.BlockSpec((1, 1, GATHER_ROWS), lambda i: (jnp.minimum(i + 1, n_blocks - 1), 0, 0),
                            memory_space=pltpu.SMEM)
    idx1, idx2 = _idx_blocks(pos1), _idx_blocks(pos2)
    return pl.pallas_call(
        functools.partial(_combine_kernel, final=final),
        grid=(n_blocks,),
        in_specs=[_IDX_SPEC, _IDX_SPEC, next_idx, next_idx, row_blk(d), row_blk(LANES),
                  pl.BlockSpec(memory_space=pl.ANY), pl.BlockSpec((1, d), lambda i: (0, 0))],
        out_specs=row_blk(d),
        out_shape=jax.ShapeDtypeStruct((rows, d), F32),
        scratch_shapes=[pltpu.VMEM((2, TOP_K, GATHER_ROWS, d), F32), pltpu.SemaphoreType.DMA((2, TOP_K))],
        compiler_params=_cparams("arbitrary"),
        name="moe_combine",
    )(idx1, idx2, idx1, idx2, h, wts, ys, gain)


def _moe_layer(h, g, router, wg, wu, wd, final_gain):
    rows, d = h.shape
    n_experts = router.shape[1]
    router_padded = jnp.pad(router, ((0, 0), (0, LANES - n_experts)))
    hn, idx, wts = _router(h, g, router_padded, n_experts)

    n_assign = rows * TOP_K
    e_flat = idx[:TOP_K].reshape(n_assign)
    onehot = (e_flat[None, :] == jnp.arange(n_experts, dtype=jnp.int32)[:, None]).astype(jnp.int32)
    csum = jnp.cumsum(onehot, axis=1)
    counts = csum[:, -1]
    padded = ((counts + MOE_TM - 1) // MOE_TM) * MOE_TM
    ends = jnp.cumsum(padded)
    pos = jnp.sum(onehot * (csum - 1 + (ends - padded)[:, None]), axis=0)
    n_tiles = n_assign // MOE_TM + n_experts
    tile_start = jnp.arange(n_tiles, dtype=jnp.int32) * MOE_TM
    tile_valid = (tile_start < ends[-1]).astype(jnp.int32)
    tile_expert = jnp.searchsorted(ends, jnp.minimum(tile_start, ends[-1] - 1), side="right").astype(jnp.int32)

    pos1, pos2 = pos[:rows], pos[rows:]
    xs = _dispatch(hn, pos1, pos2, ends.astype(jnp.int32), padded.astype(jnp.int32), n_tiles * MOE_TM)
    ys = _moe_ffn(tile_expert, tile_valid, xs, wg, wu, wd)
    return _combine(h, wts, ys, pos1, pos2, final_gain)


def _final_norm_kernel(h_ref, g_ref, o_ref):
    o_ref[...] = _rms(h_ref[...], g_ref[...], RMS_EPS)


def _final_norm(h, g):
    rows, d = h.shape
    return pl.pallas_call(
        _final_norm_kernel,
        grid=(rows // TB,),
        in_specs=[pl.BlockSpec((TB, d), lambda i: (i, 0)), pl.BlockSpec((1, d), lambda i: (0, 0))],
        out_specs=pl.BlockSpec((TB, d), lambda i: (i, 0)),
        out_shape=jax.ShapeDtypeStruct((rows, d), F32),
        compiler_params=_cparams("parallel"),
        name="final_norm",
    )(h, g)


def kernel(x, meta_tokens, rel_bias, norm_mix, w_in, pool_group_w, pool_scale, lambda_q1, lambda_k1, lambda_q2, lambda_k2, subln_gain, w_pool_up, w_attn_up, w_out, norm_ffn, dense_w_gate, dense_w_up, dense_w_down, moe_router, moe_w_gate, moe_w_up, moe_w_down, final_norm):
    batch, seq, d = x.shape
    depth = w_in.shape[0]
    seq_all = N_META + seq
    lp = -(-seq_all // TB) * TB
    nt = lp // TB
    pw = len(POOL_WINDOWS) * POOL_GROUP_DIM
    qw = DIFF_HEADS * 2 * DIFF_QK_DIM
    assert (batch * lp) % GATHER_ROWS == 0 and TB % HALO == 0 and TB % POOL_SUB == 0

    meta = jnp.broadcast_to(meta_tokens[None].astype(x.dtype), (batch, N_META, d))
    h = jnp.concatenate([meta, x, jnp.zeros((batch, lp - seq_all, d), x.dtype)], axis=1)
    h = h.reshape(batch * lp, d)

    dbias, sbias = _bias_tiles(rel_bias.astype(F32))
    bands = jnp.asarray(_pool_bands(), BF16)
    row = lambda a: a.reshape(1, -1).astype(F32)

    for layer in range(depth):
        u, q, k, v, gates = _inproj(h, row(norm_mix[layer]), w_in[layer].astype(BF16), pw, qw)
        lambda_init = 0.8 - 0.6 * math.exp(-0.3 * layer)
        b_out = _attention(q, k, v, dbias, sbias, row(lambda_q1[layer]), row(lambda_k1[layer]),
                           row(lambda_q2[layer]), row(lambda_k2[layer]), row(subln_gain[layer]),
                           batch=batch, lp=lp, lambda_init=lambda_init)
        h = _merge(h, u, b_out, gates, bands, pool_group_w[layer].astype(BF16), row(pool_scale[layer]),
                   w_pool_up[layer].astype(BF16), w_attn_up[layer].astype(BF16), w_out[layer].astype(BF16),
                   nt=nt)
        j = layer // 2
        if layer % 2 == 0:
            h = _dense_ffn(h, row(norm_ffn[layer]), dense_w_gate[j].astype(BF16),
                           dense_w_up[j].astype(BF16), dense_w_down[j].astype(BF16))
        else:
            h = _moe_layer(h, row(norm_ffn[layer]), moe_router[j].astype(F32), moe_w_gate[j].astype(BF16),
                           moe_w_up[j].astype(BF16), moe_w_down[j].astype(BF16),
                           row(final_norm) if layer == depth - 1 else None)

    out = h if depth % 2 == 0 else _final_norm(h, row(final_norm))
    return out.reshape(batch, lp, d)[:, N_META:seq_all]
```
